```python
import jax
import jax.numpy as jnp
from jax import lax
import numpy as np

D_MODEL = 1024
BATCH = 4
SEQ = 4096
DEPTH = 2
DEC_BATCH = 128
DEC_SEQ = 1
PAST_LEN = 2048
PAGE_SIZE = 128

EPS = 1e-6
D_CONV = D_MODEL // 2
CONV_W = 3
D_CHUNK = D_MODEL // 2
N_CHUNK_HEADS = 4
CHUNK_HD = D_CHUNK // N_CHUNK_HEADS
CHUNK = 128
HEAD_DIM = 64
N_HEADS = D_MODEL // HEAD_DIM
N_KV = 4
HPG = N_HEADS // N_KV
CMP_BLOCK = 32
CMP_STRIDE = 16
SEL_BLOCK = 64
TOP_N = 16
WINDOW = 512
QBLOCK = 128
ROPE_THETA = 10000.0
SCALE = HEAD_DIM ** -0.5
NEG_INF = -1e30
FORCE_BONUS = 1e4
D_FF = ((8 * D_MODEL + 2) // 3 + 255) // 256 * 256

kernel_name = "hybrid_conv_chunkmlp_nsa_decoder_step"


def rmsnorm(x, g):
    xf = x.astype(jnp.float32)
    y = xf * lax.rsqrt(jnp.mean(xf * xf, axis=-1, keepdims=True) + EPS)
    return (y * g.astype(jnp.float32)).astype(x.dtype)


def rope(x, pos):
    half = HEAD_DIM // 2
    inv = ROPE_THETA ** (-jnp.arange(half, dtype=jnp.float32) * 2.0 / HEAD_DIM)
    ang = pos.astype(jnp.float32)[:, None] * inv[None, :]
    shape = (pos.shape[0],) + (1,) * (x.ndim - 3) + (HEAD_DIM,)
    cos = jnp.concatenate([jnp.cos(ang), jnp.cos(ang)], -1).reshape(shape)
    sin = jnp.concatenate([jnp.sin(ang), jnp.sin(ang)], -1).reshape(shape)
    xf = x.astype(jnp.float32)
    rot = jnp.concatenate([-xf[..., half:], xf[..., :half]], -1)
    return (xf * cos + rot * sin).astype(x.dtype)


def swiglu(x, w_in, w_out):
    a, b = jnp.split(x @ w_in, 2, axis=-1)
    return (jax.nn.silu(a) * b) @ w_out


def chunk_mix(v, w_spatial, b_spatial):
    n, t, _ = v.shape
    nch = -(-t // CHUNK)
    tp = nch * CHUNK
    vp = jnp.pad(v, ((0, 0), (0, tp - t), (0, 0))).reshape(n, nch, CHUNK, N_CHUNK_HEADS, CHUNK_HD)
    causal = jnp.tril(jnp.ones((CHUNK, CHUNK), dtype=bool))
    w = jnp.where(causal[None], w_spatial, 0)
    mixed = jnp.einsum('hts,ncshd->ncthd', w, vp) + b_spatial.T[:, :, None]
    return mixed.reshape(n, tp, D_CHUNK)[:, :t]


def conv_chunk_mixer(x, conv_hist, w_in0, conv_w, norm_v, w_spatial, b_spatial, w_out0):
    t = x.shape[1]
    h = x @ w_in0
    b_gate, c_gate, h_conv, u, v = jnp.split(
        h, [D_CONV, 2 * D_CONV, 3 * D_CONV, 3 * D_CONV + D_CHUNK], axis=-1)
    z = c_gate * h_conv
    zh = jnp.concatenate([conv_hist.astype(z.dtype), z], axis=1)
    conv = conv_w[0] * zh[:, 0:t]
    for j in range(1, CONV_W):
        conv = conv + conv_w[j] * zh[:, j:j + t]
    y_conv = b_gate * conv
    vn = rmsnorm(jax.nn.gelu(v), norm_v)
    y_chunk = jax.nn.gelu(u) * chunk_mix(vn, w_spatial, b_spatial)
    y = jnp.concatenate([y_conv, y_chunk], axis=-1) @ w_out0
    return y, zh[:, -(CONV_W - 1):], vn


def nsa_project(x, w_in1):
    n, t, _ = x.shape
    h = x @ w_in1
    q_dim = N_HEADS * HEAD_DIM
    kv_dim = 6 * N_KV * HEAD_DIM
    q = h[..., :q_dim].reshape(n, t, N_KV, HPG, HEAD_DIM)
    kv = h[..., q_dim:q_dim + kv_dim].reshape(n, t, 6, N_KV, HEAD_DIM)
    gates = jax.nn.sigmoid(h[..., q_dim + kv_dim:].astype(jnp.float32)).astype(x.dtype)
    return q, kv, gates.reshape(n, t, N_KV, HPG, 3)


def compress(k, pe, w):
    n, l, g, d = k.shape
    nseg = -(-l // CMP_STRIDE)
    kp = jnp.pad(k, ((0, 0), (0, nseg * CMP_STRIDE - l), (0, 0), (0, 0)))
    seg = kp.reshape(n, nseg, CMP_STRIDE, g, d)
    r = CMP_BLOCK // CMP_STRIDE
    nc = nseg - r + 1
    blocks = jnp.concatenate([seg[:, i:i + nc] for i in range(r)], axis=2)
    return jnp.einsum('nclgd,lde->ncge', blocks + pe[:, None, :], w)


def nsa_core(q, q_rot, qpos, ck, cv, fetch, wk, wv, wpos, gates, key_len):
    b, nq = q.shape[0], q.shape[1]
    nc = ck.shape[1]
    t = qpos[:, None]
    s = jnp.einsum('bqghd,bngd->bghqn', q, ck).astype(jnp.float32) * SCALE
    cvalid = (jnp.arange(nc) * CMP_STRIDE + CMP_BLOCK - 1)[None, :] <= t
    p_cmp = jnp.where(cvalid, jax.nn.softmax(jnp.where(cvalid, s, NEG_INF), axis=-1), 0.0)
    o_cmp = jnp.einsum('bghqn,bngd->bqghd', p_cmp.astype(cv.dtype), cv)
    ns = -(-key_len // SEL_BLOCK)
    cstart = jnp.arange(nc) * CMP_STRIDE
    sstart = jnp.arange(ns) * SEL_BLOCK
    overlap = ((cstart[:, None] < sstart[None, :] + SEL_BLOCK)
               & (cstart[:, None] + CMP_BLOCK > sstart[None, :])).astype(jnp.float32)
    imp = jnp.einsum('bghqn,nj->bgqj', p_cmp, overlap)
    cur = (qpos // SEL_BLOCK)[:, None]
    jj = jnp.arange(ns)[None, :]
    forced = (jj == 0) | (jj == cur) | (jj == cur - 1)
    svalid = sstart[None, :] <= t
    score = jnp.where(svalid, imp + jnp.where(forced, FORCE_BONUS, 0.0), NEG_INF)
    _, idx = lax.top_k(score, min(TOP_N, ns))
    kpos = (idx[..., None] * SEL_BLOCK + jnp.arange(SEL_BLOCK)).reshape(b, N_KV, nq, -1)
    k_sel, v_sel = fetch(kpos)
    s = jnp.einsum('bqghd,bgqkd->bghqk', q_rot, k_sel).astype(jnp.float32) * SCALE
    smask = (kpos <= qpos[None, None, :, None])[:, :, None]
    p_slc = jax.nn.softmax(jnp.where(smask, s, NEG_INF), axis=-1)
    o_slc = jnp.einsum('bghqk,bgqkd->bqghd', p_slc.astype(v_sel.dtype), v_sel)
    s = jnp.einsum('bqghd,bwgd->bghqw', q_rot, wk).astype(jnp.float32) * SCALE
    dist = t - wpos[None, :]
    wmask = (wpos[None, :] >= 0) & (dist >= 0) & (dist < WINDOW)
    p_win = jax.nn.softmax(jnp.where(wmask, s, NEG_INF), axis=-1)
    o_win = jnp.einsum('bghqw,bwgd->bqghd', p_win.astype(wv.dtype), wv)
    o = gates[..., 0:1] * o_cmp + gates[..., 1:2] * o_slc + gates[..., 2:3] * o_win
    return o.reshape(b, nq, N_HEADS * HEAD_DIM)


def nsa_prompt(x, w_in1, pe_cmp, w_cmp, w_out1):
    n, t, _ = x.shape
    pos = jnp.arange(t)
    q, kv, gates = nsa_project(x, w_in1)
    q_rot = rope(q, pos)
    k_cmp, v_cmp = kv[:, :, 0], kv[:, :, 1]
    k_slc, v_slc = rope(kv[:, :, 2], pos), kv[:, :, 3]
    k_win, v_win = rope(kv[:, :, 4], pos), kv[:, :, 5]
    ck = compress(k_cmp, pe_cmp[0], w_cmp[0])
    cv = compress(v_cmp, pe_cmp[1], w_cmp[1])
    pad = ((0, 0), (WINDOW, 0), (0, 0), (0, 0))
    kw_pad = jnp.pad(k_win, pad)
    vw_pad = jnp.pad(v_win, pad)
    bi = jnp.arange(n)[:, None, None, None]
    gi = jnp.arange(N_KV)[None, :, None, None]

    def fetch(kpos):
        p = jnp.clip(kpos, 0, t - 1)
        return k_slc[bi, p, gi], v_slc[bi, p, gi]

    def block(i):
        qs = i * QBLOCK
        qpos = qs + jnp.arange(QBLOCK)
        wpos = qs - WINDOW + jnp.arange(WINDOW + QBLOCK)
        qb = lax.dynamic_slice_in_dim(q, qs, QBLOCK, axis=1)
        qrb = lax.dynamic_slice_in_dim(q_rot, qs, QBLOCK, axis=1)
        gb = lax.dynamic_slice_in_dim(gates, qs, QBLOCK, axis=1)
        wk = lax.dynamic_slice_in_dim(kw_pad, qs, WINDOW + QBLOCK, axis=1)
        wv = lax.dynamic_slice_in_dim(vw_pad, qs, WINDOW + QBLOCK, axis=1)
        return nsa_core(qb, qrb, qpos, ck, cv, fetch, wk, wv, wpos, gb, t)

    o = lax.map(block, jnp.arange(t // QBLOCK))
    o = jnp.moveaxis(o, 0, 1).reshape(n, t, N_HEADS * HEAD_DIM)
    y = o @ w_out1
    kv_rows = jnp.stack([k_cmp, v_cmp, k_slc, v_slc], axis=2)
    win_rows = jnp.stack([k_win, v_win], axis=2)[:, t - min(WINDOW, t):]
    return y, kv_rows, win_rows


def nsa_sample(x, cache_kv, cache_win, page_table, w_in1, pe_cmp, w_cmp, w_out1):
    n, s_len, _ = x.shape
    past_len = page_table.shape[1] * PAGE_SIZE
    wb = cache_win.shape[1]
    qpos = past_len + jnp.arange(s_len)
    q, kv, gates = nsa_project(x, w_in1)
    q_rot = rope(q, qpos)
    k_cmp, v_cmp = kv[:, :, 0], kv[:, :, 1]
    k_slc, v_slc = rope(kv[:, :, 2], qpos), kv[:, :, 3]
    k_win, v_win = rope(kv[:, :, 4], qpos), kv[:, :, 5]
    past_cmp = cache_kv[page_table, :, 0:2].reshape(n, past_len, 2, N_KV, HEAD_DIM)
    ck = compress(jnp.concatenate([past_cmp[:, :, 0], k_cmp], axis=1), pe_cmp[0], w_cmp[0])
    cv = compress(jnp.concatenate([past_cmp[:, :, 1], v_cmp], axis=1), pe_cmp[1], w_cmp[1])
    bi = jnp.arange(n)[:, None, None, None]
    gi = jnp.arange(N_KV)[None, :, None, None]

    def fetch(kpos):
        pc = jnp.clip(kpos, 0, past_len - 1)
        phys = page_table[bi, pc // PAGE_SIZE]
        off = pc % PAGE_SIZE
        kp = cache_kv[phys, off, 2, gi]
        vp = cache_kv[phys, off, 3, gi]
        pn = jnp.clip(kpos - past_len, 0, s_len - 1)
        is_past = (kpos < past_len)[..., None]
        return jnp.where(is_past, kp, k_slc[bi, pn, gi]), jnp.where(is_past, vp, v_slc[bi, pn, gi])

    wk = jnp.concatenate([cache_win[:, :, 0], k_win], axis=1)
    wv = jnp.concatenate([cache_win[:, :, 1], v_win], axis=1)
    wpos = jnp.concatenate([past_len - wb + jnp.arange(wb), qpos])
    o = nsa_core(q, q_rot, qpos, ck, cv, fetch, wk, wv, wpos, gates, past_len + s_len)
    y = o @ w_out1
    kv_rows = jnp.stack([k_cmp, v_cmp, k_slc, v_slc], axis=2)
    win_rows = jnp.stack([k_win, v_win], axis=2)
    return y, kv_rows, win_rows


def setup_inputs(seed: int = 0) -> dict:
    key = jax.random.key(seed)
    ks = jax.random.split(key, 24)
    f32 = jnp.float32

    def nrm(k, shape, scale):
        return jax.random.normal(k, shape, f32) * scale

    n_pages = PAST_LEN // PAGE_SIZE
    n_used = DEC_BATCH * n_pages
    n_phys = n_used + max(1, n_used // 4)
    page_table = jax.random.permutation(ks[0], n_phys)[:n_used].reshape(DEC_BATCH, n_pages).astype(jnp.int32)
    win_buf = min(WINDOW, PAST_LEN)
    in1_cols = N_HEADS * HEAD_DIM + 6 * N_KV * HEAD_DIM + 3 * N_HEADS
    return {
        "x_prompt": nrm(ks[1], (BATCH, SEQ, D_MODEL), 1.0),
        "x_sample": nrm(ks[2], (DEC_BATCH, DEC_SEQ, D_MODEL), 1.0),
        "state_conv": nrm(ks[3], (DEC_BATCH, CONV_W - 1, D_CONV), 1.0),
        "cache_kv": nrm(ks[4], (n_phys, PAGE_SIZE, 4, N_KV, HEAD_DIM), 1.0),
        "cache_win": nrm(ks[5], (DEC_BATCH, win_buf, 2, N_KV, HEAD_DIM), 1.0),
        "page_table": page_table,
        "norm_mix": 1.0 + nrm(ks[6], (DEPTH, D_MODEL), 0.05),
        "norm_ffn": 1.0 + nrm(ks[7], (DEPTH, D_MODEL), 0.05),
        "norm_final": 1.0 + nrm(ks[8], (D_MODEL,), 0.05),
        "w_in0": nrm(ks[9], (D_MODEL, 3 * D_CONV + 2 * D_CHUNK), D_MODEL ** -0.5),
        "conv_w": nrm(ks[10], (CONV_W, D_CONV), CONV_W ** -0.5),
        "norm_v": 1.0 + nrm(ks[11], (D_CHUNK,), 0.05),
        "w_spatial": nrm(ks[12], (N_CHUNK_HEADS, CHUNK, CHUNK), CHUNK ** -0.5),
        "b_spatial": 1.0 + nrm(ks[13], (N_CHUNK_HEADS, CHUNK), 0.1),
        "w_out0": nrm(ks[14], (D_CONV + D_CHUNK, D_MODEL), (D_CONV + D_CHUNK) ** -0.5),
        "w_in1": nrm(ks[15], (D_MODEL, in1_cols), D_MODEL ** -0.5),
        "pe_cmp": nrm(ks[16], (2, CMP_BLOCK, HEAD_DIM), 0.1),
        "w_cmp": nrm(ks[17], (2, CMP_BLOCK, HEAD_DIM, HEAD_DIM), (CMP_BLOCK * HEAD_DIM) ** -0.5),
        "w_out1": nrm(ks[18], (N_HEADS * HEAD_DIM, D_MODEL), (N_HEADS * HEAD_DIM) ** -0.5),
        "w_ffn_in": nrm(ks[19], (DEPTH, D_MODEL, 2 * D_FF), D_MODEL ** -0.5),
        "w_ffn_out": nrm(ks[20], (DEPTH, D_FF, D_MODEL), D_FF ** -0.5),
    }


def reference(x_prompt, x_sample, state_conv, cache_kv, cache_win, page_table,
              norm_mix, norm_ffn, norm_final, w_in0, conv_w, norm_v, w_spatial, b_spatial, w_out0,
              w_in1, pe_cmp, w_cmp, w_out1, w_ffn_in, w_ffn_out):
    xp, xs = x_prompt, x_sample
    for layer in range(DEPTH):
        hp = rmsnorm(xp, norm_mix[layer])
        hs = rmsnorm(xs, norm_mix[layer])
        if layer % 2 == 0:
            zero_hist = jnp.zeros((xp.shape[0], CONV_W - 1, D_CONV), xp.dtype)
            yp, conv_state_prompt, _ = conv_chunk_mixer(
                hp, zero_hist, w_in0, conv_w, norm_v, w_spatial, b_spatial, w_out0)
            ys, conv_state_sample, chunk_v_sample = conv_chunk_mixer(
                hs, state_conv, w_in0, conv_w, norm_v, w_spatial, b_spatial, w_out0)
        else:
            yp, kv_prompt, win_prompt = nsa_prompt(hp, w_in1, pe_cmp, w_cmp, w_out1)
            ys, kv_sample, win_sample = nsa_sample(
                hs, cache_kv, cache_win, page_table, w_in1, pe_cmp, w_cmp, w_out1)
        xp = xp + yp
        xs = xs + ys
        xp = xp + swiglu(rmsnorm(xp, norm_ffn[layer]), w_ffn_in[layer], w_ffn_out[layer])
        xs = xs + swiglu(rmsnorm(xs, norm_ffn[layer]), w_ffn_in[layer], w_ffn_out[layer])
    y_prompt = rmsnorm(xp, norm_final)
    y_sample = rmsnorm(xs, norm_final)
    return (y_prompt, y_sample, conv_state_prompt, conv_state_sample, chunk_v_sample,
            kv_prompt, kv_sample, win_prompt, win_sample)
```

```python
import functools

import numpy as np
import jax
import jax.numpy as jnp
from jax import lax
from jax.experimental import pallas as pl
from jax.experimental.pallas import tpu as pltpu

F32 = jnp.float32
BF16 = jnp.bfloat16

EPS = 1e-6
D_MODEL = 1024
D_CONV = 512
D_CHUNK = 512
N_CHUNK_HEADS = 4
CHUNK = 128
HEAD_DIM = 64
N_HEADS = 16
N_KV = 4
HPG = 4
KV_W = N_KV * HEAD_DIM
CMP_BLOCK = 32
CMP_STRIDE = 16
SEL_BLOCK = 64
TOP_N = 16
WINDOW = 512
QBLOCK = 128
PAGE_SIZE = 128
ROPE_THETA = 10000.0
SCALE = HEAD_DIM ** -0.5
NEG_INF = -1e30
FORCE_BONUS = 1e4
D_FF = 2816

LANES = 128
SUBLANES = 8
VMEM_LIMIT = 56 * 1024 * 1024


def _params(sem):
    return pltpu.CompilerParams(dimension_semantics=sem, vmem_limit_bytes=VMEM_LIMIT)


def _rms(x, g):
    ms = jnp.mean(x * x, axis=-1, keepdims=True)
    return x * lax.rsqrt(ms + EPS) * g


def _dot(a, b):
    return jnp.dot(a, b, preferred_element_type=F32)


def _dot_nt(a, b):
    return lax.dot_general(a, b, (((1,), (1,)), ((), ())), preferred_element_type=F32)


def _l0_prompt_body(x_ref, g_ref, win_ref, cw_ref, nv_ref, wsp_ref, bsp_ref, wout_ref,
                    y_ref, ztail_ref, carry_ref, ymix_ref):
    t = pl.program_id(1)
    x = x_ref[0]
    tr = x.shape[0]
    h = _dot(_rms(x, g_ref[...]).astype(BF16), win_ref[...])
    b_gate = h[:, 0:D_CONV]
    c_gate = h[:, D_CONV:2 * D_CONV]
    h_conv = h[:, 2 * D_CONV:3 * D_CONV]
    u = h[:, 3 * D_CONV:3 * D_CONV + D_CHUNK]
    v = h[:, 3 * D_CONV + D_CHUNK:]
    z = c_gate * h_conv

    @pl.when(t == 0)
    def _():
        carry_ref[...] = jnp.zeros_like(carry_ref)

    prev = carry_ref[...]
    row = lax.broadcasted_iota(jnp.int32, z.shape, 0)
    z1 = jnp.where(row == 0, prev[7:8], pltpu.roll(z, 1, axis=0))
    z2 = jnp.where(row == 0, prev[6:7], jnp.where(row == 1, prev[7:8], pltpu.roll(z, 2, axis=0)))
    cw = cw_ref[...]
    conv = cw[0:1] * z2 + cw[1:2] * z1 + cw[2:3] * z
    carry_ref[...] = z[tr - SUBLANES:]
    ztail_ref[0] = z[tr - SUBLANES:]
    ymix_ref[:, 0:D_CONV] = (b_gate * conv).astype(BF16)

    vn = _rms(jax.nn.gelu(v), nv_ref[...]).astype(BF16)
    gu = jax.nn.gelu(u)
    for c in range(tr // CHUNK):
        rs = slice(c * CHUNK, (c + 1) * CHUNK)
        for hd in range(N_CHUNK_HEADS):
            cs = slice(hd * LANES, (hd + 1) * LANES)
            mixed = _dot(wsp_ref[hd], vn[rs, cs]) + bsp_ref[:, cs]
            ymix_ref[rs, D_CONV + hd * LANES:D_CONV + (hd + 1) * LANES] = (gu[rs, cs] * mixed).astype(BF16)
    y_ref[0] = x + _dot(ymix_ref[...], wout_ref[...])


def _l0_prompt(x, g, w_in0, cw8, nv, wsp, bsp, w_out0, tr=512):
    b, t, d = x.shape
    const = lambda *shape: pl.BlockSpec(shape, lambda i, j: (0,) * len(shape))
    return pl.pallas_call(
        _l0_prompt_body,
        grid=(b, t // tr),
        in_specs=[
            pl.BlockSpec((1, tr, d), lambda i, j: (i, j, 0)),
            const(1, d), const(*w_in0.shape), const(*cw8.shape), const(1, D_CHUNK),
            const(*wsp.shape), const(*bsp.shape), const(*w_out0.shape),
        ],
        out_specs=[
            pl.BlockSpec((1, tr, d), lambda i, j: (i, j, 0)),
            pl.BlockSpec((1, SUBLANES, D_CONV), lambda i, j: (i, 0, 0)),
        ],
        out_shape=[jax.ShapeDtypeStruct((b, t, d), F32),
                   jax.ShapeDtypeStruct((b, SUBLANES, D_CONV), F32)],
        scratch_shapes=[pltpu.VMEM((SUBLANES, D_CONV), F32), pltpu.VMEM((tr, D_MODEL), BF16)],
        compiler_params=_params(("arbitrary", "arbitrary")),
        name="l0_mixer_prompt",
    )(x, g, w_in0, cw8, nv, wsp, bsp, w_out0)


def _l0_sample_body(x_ref, h0_ref, h1_ref, g_ref, win_ref, cw_ref, nv_ref, w00_ref, b0_ref, wout_ref,
                    y_ref, z_ref, vn_ref):
    x = x_ref[...]
    h = _dot(_rms(x, g_ref[...]).astype(BF16), win_ref[...])
    b_gate = h[:, 0:D_CONV]
    c_gate = h[:, D_CONV:2 * D_CONV]
    h_conv = h[:, 2 * D_CONV:3 * D_CONV]
    u = h[:, 3 * D_CONV:3 * D_CONV + D_CHUNK]
    v = h[:, 3 * D_CONV + D_CHUNK:]
    z = c_gate * h_conv
    cw = cw_ref[...]
    conv = cw[0:1] * h0_ref[...] + cw[1:2] * h1_ref[...] + cw[2:3] * z
    vn = _rms(jax.nn.gelu(v), nv_ref[...])
    mixed = w00_ref[...] * vn + b0_ref[...]
    ymix = jnp.concatenate([b_gate * conv, jax.nn.gelu(u) * mixed], axis=1).astype(BF16)
    y_ref[...] = x + _dot(ymix, wout_ref[...])
    z_ref[...] = z
    vn_ref[...] = vn


def _l0_sample(x, h0, h1, g, w_in0, cw8, nv, w00, b0, w_out0):
    n, d = x.shape
    args = (x, h0, h1, g, w_in0, cw8, nv, w00, b0, w_out0)
    return pl.pallas_call(
        _l0_sample_body,
        grid=(1,),
        in_specs=[pl.BlockSpec(a.shape, lambda i, nd=a.ndim: (0,) * nd) for a in args],
        out_specs=[pl.BlockSpec((n, d), lambda i: (0, 0)),
                   pl.BlockSpec((n, D_CONV), lambda i: (0, 0)),
                   pl.BlockSpec((n, D_CHUNK), lambda i: (0, 0))],
        out_shape=[jax.ShapeDtypeStruct((n, d), F32),
                   jax.ShapeDtypeStruct((n, D_CONV), F32),
                   jax.ShapeDtypeStruct((n, D_CHUNK), F32)],
        compiler_params=_params(("arbitrary",)),
        name="l0_mixer_sample",
    )(*args)


def _ffn_body(x_ref, g_ref, wa_ref, wb_ref, wo_ref, gf_ref, o_ref, xn_ref, acc_ref, *, final_norm):
    j = pl.program_id(1)

    @pl.when(j == 0)
    def _():
        x = x_ref[...]
        xn_ref[...] = _rms(x, g_ref[...]).astype(BF16)
        acc_ref[...] = x

    xn = xn_ref[...]
    a = _dot(xn, wa_ref[...])
    b = _dot(xn, wb_ref[...])
    act = (jax.nn.silu(a) * b).astype(BF16)
    acc_ref[...] += _dot(act, wo_ref[...])

    @pl.when(j == pl.num_programs(1) - 1)
    def _():
        r = acc_ref[...]
        o_ref[...] = _rms(r, gf_ref[...]) if final_norm else r


def _ffn(x, g, w_in, w_out, gf, final_norm, tr, nf=2):
    rows, d = x.shape
    tf = D_FF // nf
    return pl.pallas_call(
        functools.partial(_ffn_body, final_norm=final_norm),
        grid=(rows // tr, nf),
        in_specs=[
            pl.BlockSpec((tr, d), lambda i, j: (i, 0)),
            pl.BlockSpec((1, d), lambda i, j: (0, 0)),
            pl.BlockSpec((d, tf), lambda i, j: (0, j)),
            pl.BlockSpec((d, tf), lambda i, j: (0, nf + j)),
            pl.BlockSpec((tf, d), lambda i, j: (j, 0)),
            pl.BlockSpec((1, d), lambda i, j: (0, 0)),
        ],
        out_specs=pl.BlockSpec((tr, d), lambda i, j: (i, 0)),
        out_shape=jax.ShapeDtypeStruct((rows, d), F32),
        scratch_shapes=[pltpu.VMEM((tr, d), BF16), pltpu.VMEM((tr, d), F32)],
        compiler_params=_params(("arbitrary", "arbitrary")),
        name="ffn_final" if final_norm else "ffn",
    )(x, g, w_in, w_in, w_out, gf)


def _rope_cols(xs, cos, sin_signed, first_half):
    outs = []
    for c in range(xs.shape[1] // LANES):
        ch = xs[:, c * LANES:(c + 1) * LANES]
        rot = jnp.where(first_half, pltpu.roll(ch, LANES - HEAD_DIM // 2, axis=1),
                        pltpu.roll(ch, HEAD_DIM // 2, axis=1))
        outs.append(ch * cos + rot * sin_signed)
    return outs[0] if len(outs) == 1 else jnp.concatenate(outs, axis=1)


def _l1_proj_body(x_ref, g_ref, w_ref, wg_ref, cos_ref, sin_ref, *outs, prompt):
    x = x_ref[0] if prompt else x_ref[...]
    tr = x.shape[0]
    xn = _rms(x, g_ref[...]).astype(BF16)
    h = _dot(xn, w_ref[...])
    gates = jax.nn.sigmoid(_dot(xn, wg_ref[...]))
    cos = cos_ref[...]
    sin_signed = sin_ref[...]
    lane = lax.broadcasted_iota(jnp.int32, cos.shape, 1)
    first_half = (lane % HEAD_DIM) < (HEAD_DIM // 2)
    rope = functools.partial(_rope_cols, cos=cos, sin_signed=sin_signed, first_half=first_half)

    q = h[:, 0:D_MODEL] * SCALE
    q_rot = rope(q)
    o = D_MODEL
    k_cmp_v_cmp = h[:, o:o + 2 * KV_W]
    k_slc = rope(h[:, o + 2 * KV_W:o + 3 * KV_W])
    v_slc = h[:, o + 3 * KV_W:o + 4 * KV_W]
    k_win = rope(h[:, o + 4 * KV_W:o + 5 * KV_W])
    v_win = h[:, o + 5 * KV_W:o + 6 * KV_W]

    if not prompt:
        q_ref, qr_ref, g_out, kv_ref, wn_ref = outs
        q_ref[...] = q
        qr_ref[...] = q_rot
        g_out[...] = gates
        kv_ref[:, 0:2 * KV_W] = k_cmp_v_cmp
        kv_ref[:, 2 * KV_W:3 * KV_W] = k_slc
        kv_ref[:, 3 * KV_W:4 * KV_W] = v_slc
        wn_ref[:, 0:KV_W] = k_win
        wn_ref[:, KV_W:2 * KV_W] = v_win
        return

    qt_ref, qrt_ref, gt_ref, kv_ref, wn_ref, kslc_ref, kwin_ref, vslct_ref, vwint_ref = outs
    kv_ref[0, :, 0:2 * KV_W] = k_cmp_v_cmp
    kv_ref[0, :, 2 * KV_W:3 * KV_W] = k_slc
    kv_ref[0, :, 3 * KV_W:4 * KV_W] = v_slc
    wn_ref[0, :, 0:KV_W] = k_win
    wn_ref[0, :, KV_W:2 * KV_W] = v_win
    qt = q.T.astype(BF16)
    qrt = q_rot.T.astype(BF16)
    gt = gates.T
    vst = v_slc.T.astype(BF16)
    vwt = v_win.T.astype(BF16)
    for r in range(tr // QBLOCK):
        cs = slice(r * QBLOCK, (r + 1) * QBLOCK)
        qt_ref[0, r] = qt[:, cs]
        qrt_ref[0, r] = qrt[:, cs]
        gt_ref[0, r] = gt[0:3 * N_HEADS, cs]
        for gi in range(N_KV):
            vslct_ref[0, gi, r] = vst[gi * HEAD_DIM:(gi + 1) * HEAD_DIM, cs]
            vwint_ref[0, gi, r] = vwt[gi * HEAD_DIM:(gi + 1) * HEAD_DIM, cs]
    for gi in range(N_KV):
        kslc_ref[0, gi] = k_slc[:, gi * HEAD_DIM:(gi + 1) * HEAD_DIM].astype(BF16)
        kwin_ref[0, gi] = k_win[:, gi * HEAD_DIM:(gi + 1) * HEAD_DIM].astype(BF16)


def _l1_proj_prompt(x, g, w_qkv, w_gate, cos, sin_signed, tr=512):
    b, t, d = x.shape
    nqb = t // QBLOCK
    rb = tr // QBLOCK
    const = lambda *shape: pl.BlockSpec(shape, lambda i, j: (0,) * len(shape))
    out_shape = [
        jax.ShapeDtypeStruct((b, nqb, D_MODEL, QBLOCK), BF16),
        jax.ShapeDtypeStruct((b, nqb, D_MODEL, QBLOCK), BF16),
        jax.ShapeDtypeStruct((b, nqb, 3 * N_HEADS, QBLOCK), F32),
        jax.ShapeDtypeStruct((b, t, 4 * KV_W), F32),
        jax.ShapeDtypeStruct((b, t, 2 * KV_W), F32),
        jax.ShapeDtypeStruct((b, N_KV, t, HEAD_DIM), BF16),
        jax.ShapeDtypeStruct((b, N_KV, t, HEAD_DIM), BF16),
        jax.ShapeDtypeStruct((b, N_KV, nqb, HEAD_DIM, QBLOCK), BF16),
        jax.ShapeDtypeStruct((b, N_KV, nqb, HEAD_DIM, QBLOCK), BF16),
    ]
    out_specs = [
        pl.BlockSpec((1, rb, D_MODEL, QBLOCK), lambda i, j: (i, j, 0, 0)),
        pl.BlockSpec((1, rb, D_MODEL, QBLOCK), lambda i, j: (i, j, 0, 0)),
        pl.BlockSpec((1, rb, 3 * N_HEADS, QBLOCK), lambda i, j: (i, j, 0, 0)),
        pl.BlockSpec((1, tr, 4 * KV_W), lambda i, j: (i, j, 0)),
        pl.BlockSpec((1, tr, 2 * KV_W), lambda i, j: (i, j, 0)),
        pl.BlockSpec((1, N_KV, tr, HEAD_DIM), lambda i, j: (i, 0, j, 0)),
        pl.BlockSpec((1, N_KV, tr, HEAD_DIM), lambda i, j: (i, 0, j, 0)),
        pl.BlockSpec((1, N_KV, rb, HEAD_DIM, QBLOCK), lambda i, j: (i, 0, j, 0, 0)),
        pl.BlockSpec((1, N_KV, rb, HEAD_DIM, QBLOCK), lambda i, j: (i, 0, j, 0, 0)),
    ]
    return pl.pallas_call(
        functools.partial(_l1_proj_body, prompt=True),
        grid=(b, t // tr),
        in_specs=[
            pl.BlockSpec((1, tr, d), lambda i, j: (i, j, 0)),
            const(1, d), const(*w_qkv.shape), const(*w_gate.shape),
            pl.BlockSpec((tr, LANES), lambda i, j: (j, 0)),
            pl.BlockSpec((tr, LANES), lambda i, j: (j, 0)),
        ],
        out_specs=out_specs,
        out_shape=out_shape,
        compiler_params=_params(("arbitrary", "arbitrary")),
        name="l1_proj_prompt",
    )(x, g, w_qkv, w_gate, cos, sin_signed)


def _l1_proj_sample(x, g, w_qkv, w_gate, cos, sin_signed):
    n, d = x.shape
    args = (x, g, w_qkv, w_gate, cos, sin_signed)
    widths = (D_MODEL, D_MODEL, LANES, 4 * KV_W, 2 * KV_W)
    return pl.pallas_call(
        functools.partial(_l1_proj_body, prompt=False),
        grid=(1,),
        in_specs=[pl.BlockSpec(a.shape, lambda i, nd=a.ndim: (0,) * nd) for a in args],
        out_specs=[pl.BlockSpec((n, w), lambda i: (0, 0)) for w in widths],
        out_shape=[jax.ShapeDtypeStruct((n, w), F32) for w in widths],
        compiler_params=_params(("arbitrary",)),
        name="l1_proj_sample",
    )(*args)


def _compress_prompt_body(x_ref, pe_ref, wk_ref, wv_ref, ck_ref, cvt_ref, acc_ref):
    l = pl.program_id(1)
    x = x_ref[0]
    parts = []
    for kind, w_ref in ((0, wk_ref), (1, wv_ref)):
        xs = x[:, kind * KV_W:(kind + 1) * KV_W]
        for half in range(2):
            pe = pe_ref[kind, pl.ds(half * CMP_STRIDE + l, 1), :]
            parts.append(_dot((xs + pe).astype(BF16), w_ref[half, 0]))

    @pl.when(l == 0)
    def _():
        for i, p in enumerate(parts):
            acc_ref[i] = p

    @pl.when(l > 0)
    def _():
        for i, p in enumerate(parts):
            acc_ref[i] += p

    @pl.when(l == pl.num_programs(1) - 1)
    def _():
        nseg = x.shape[0]
        ck = acc_ref[0] + pltpu.roll(acc_ref[1], nseg - 1, axis=0)
        cv = acc_ref[2] + pltpu.roll(acc_ref[3], nseg - 1, axis=0)
        cvt = cv.T
        for gi in range(N_KV):
            ck_ref[0, gi] = ck[:, gi * HEAD_DIM:(gi + 1) * HEAD_DIM].astype(BF16)
            cvt_ref[0, gi] = cvt[gi * HEAD_DIM:(gi + 1) * HEAD_DIM, :].astype(BF16)


def _compress_prompt(kv, pe_t, wk_bd, wv_bd):
    b, t, _ = kv.shape
    nseg = t // CMP_STRIDE
    kvs = kv.reshape(b, nseg, CMP_STRIDE * 4 * KV_W)
    return pl.pallas_call(
        _compress_prompt_body,
        grid=(b, CMP_STRIDE),
        in_specs=[
            pl.BlockSpec((1, nseg, 2 * KV_W), lambda i, l: (i, 0, 2 * l)),
            pl.BlockSpec(pe_t.shape, lambda i, l: (0, 0, 0)),
            pl.BlockSpec((2, 1, KV_W, KV_W), lambda i, l: (0, l, 0, 0)),
            pl.BlockSpec((2, 1, KV_W, KV_W), lambda i, l: (0, l, 0, 0)),
        ],
        out_specs=[
            pl.BlockSpec((1, N_KV, nseg, HEAD_DIM), lambda i, l: (i, 0, 0, 0)),
            pl.BlockSpec((1, N_KV, HEAD_DIM, nseg), lambda i, l: (i, 0, 0, 0)),
        ],
        out_shape=[jax.ShapeDtypeStruct((b, N_KV, nseg, HEAD_DIM), BF16),
                   jax.ShapeDtypeStruct((b, N_KV, HEAD_DIM, nseg), BF16)],
        scratch_shapes=[pltpu.VMEM((4, nseg, KV_W), F32)],
        compiler_params=_params(("arbitrary", "arbitrary")),
        name="compress_prompt",
    )(kvs, pe_t, wk_bd, wv_bd)


def _online_block(s, mask, vt, carry):
    m, l, acc = carry
    s = jnp.where(mask, s, NEG_INF)
    m_new = jnp.maximum(m, jnp.max(s, axis=0, keepdims=True))
    alpha = jnp.exp(m - m_new)
    p = jnp.exp(s - m_new)
    l = alpha * l + jnp.sum(p, axis=0, keepdims=True)
    acc = alpha * acc + _dot(vt, p.astype(BF16))
    return m_new, l, acc


def _nsa_prompt_body(qt_ref, qrt_ref, gt_ref, ck_ref, cvt_ref, kslc_ref, vslct_ref, kwin_ref, vwint_ref,
                     ovt_ref, x_ref, wout_ref, y_ref, ot_ref, score_ref, sel_ref):
    i = pl.program_id(1)
    qs = i * QBLOCK
    nlane = HPG * QBLOCK
    nc = ck_ref.shape[2]
    ns = ovt_ref.shape[0]

    kr = lax.broadcasted_iota(jnp.int32, (QBLOCK, nlane), 0)
    qc = lax.broadcasted_iota(jnp.int32, (QBLOCK, nlane), 1) % QBLOCK
    causal = kr <= qc
    upper = kr > qc
    half = kr < SEL_BLOCK

    for g in range(N_KV):
        heads = [g * HPG + h for h in range(HPG)]
        q_t = jnp.concatenate([qt_ref[0, 0, hd * HEAD_DIM:(hd + 1) * HEAD_DIM, :] for hd in heads], axis=1)
        qr_t = jnp.concatenate([qrt_ref[0, 0, hd * HEAD_DIM:(hd + 1) * HEAD_DIM, :] for hd in heads], axis=1)

        sc = _dot(ck_ref[0, g], q_t)
        c_idx = lax.broadcasted_iota(jnp.int32, (nc, nlane), 0)
        qpos = qs + lax.broadcasted_iota(jnp.int32, (nc, nlane), 1) % QBLOCK
        cvalid = c_idx * CMP_STRIDE + (CMP_BLOCK - 1) <= qpos
        sm = jnp.where(cvalid, sc, NEG_INF)
        e = jnp.where(cvalid, jnp.exp(sm - jnp.max(sm, axis=0, keepdims=True)), 0.0)
        den = jnp.sum(e, axis=0, keepdims=True)
        p_cmp = (e / jnp.where(den > 0.0, den, 1.0)).astype(BF16)
        o_cmp = _dot(cvt_ref[0, g], p_cmp)
        imp4 = _dot(ovt_ref[...], p_cmp)
        imp = (imp4[:, 0:QBLOCK] + imp4[:, QBLOCK:2 * QBLOCK]
               + imp4[:, 2 * QBLOCK:3 * QBLOCK] + imp4[:, 3 * QBLOCK:4 * QBLOCK])

        j_idx = lax.broadcasted_iota(jnp.int32, (ns, QBLOCK), 0)
        qp = qs + lax.broadcasted_iota(jnp.int32, (ns, QBLOCK), 1)
        cur = qp // SEL_BLOCK
        forced = (j_idx == 0) | (j_idx == cur) | (j_idx == cur - 1)
        svalid = j_idx * SEL_BLOCK <= qp
        score = jnp.where(svalid, imp + jnp.where(forced, FORCE_BONUS, 0.0), NEG_INF)
        score_ref[...] = score

        def rank_step(jp, cnt):
            rb = jnp.broadcast_to(score_ref[pl.ds(jp, 1), :], (ns, QBLOCK))
            tie = jnp.where(jp < j_idx, 1, 0)
            return cnt + jnp.where(rb > score, 1, jnp.where(rb == score, tie, 0))

        n_cand = jnp.minimum(2 * i + 2, ns)
        cnt = lax.fori_loop(0, n_cand, rank_step, jnp.zeros((ns, QBLOCK), jnp.int32))
        sel_ref[...] = jnp.concatenate([jnp.where(cnt < TOP_N, 1.0, 0.0)] * HPG, axis=1)

        def sel_mask(kb):
            r0 = jnp.broadcast_to(sel_ref[pl.ds(2 * kb, 1), :], (QBLOCK, nlane))
            r1 = jnp.broadcast_to(sel_ref[pl.ds(2 * kb + 1, 1), :], (QBLOCK, nlane))
            return jnp.where(half, r0, r1) > 0.5

        def slc_step(kb, carry, extra=None):
            k0 = pl.multiple_of(kb * QBLOCK, QBLOCK)
            s = _dot(kslc_ref[0, g, pl.ds(k0, QBLOCK), :], qr_t)
            mask = sel_mask(kb)
            if extra is not None:
                mask = mask & extra
            return _online_block(s, mask, vslct_ref[0, g, kb], carry)

        init = (jnp.full((1, nlane), NEG_INF, F32), jnp.zeros((1, nlane), F32),
                jnp.zeros((HEAD_DIM, nlane), F32))
        carry = slc_step(i, init, causal)
        m_s, l_s, acc_s = lax.fori_loop(0, i, slc_step, carry)
        o_slc = acc_s / l_s

        def win_step(kb, carry, mask):
            k0 = pl.multiple_of(kb * QBLOCK, QBLOCK)
            s = _dot(kwin_ref[0, g, pl.ds(k0, QBLOCK), :], qr_t)
            return _online_block(s, mask, vwint_ref[0, g, kb], carry)

        nfull = WINDOW // QBLOCK - 1
        all_true = kr >= 0
        carry = win_step(i, init, causal)
        carry = lax.fori_loop(jnp.maximum(i - nfull, 0), i,
                              lambda kb, c: win_step(kb, c, all_true), carry)
        far = i - WINDOW // QBLOCK
        m_w, l_w, acc_w = win_step(jnp.maximum(far, 0), carry, upper & (far >= 0))
        o_win = acc_w / l_w

        for h, hd in enumerate(heads):
            cs = slice(h * QBLOCK, (h + 1) * QBLOCK)
            g_cmp = gt_ref[0, 0, 3 * hd:3 * hd + 1, :]
            g_slc = gt_ref[0, 0, 3 * hd + 1:3 * hd + 2, :]
            g_win = gt_ref[0, 0, 3 * hd + 2:3 * hd + 3, :]
            ot_ref[hd * HEAD_DIM:(hd + 1) * HEAD_DIM, :] = (
                g_cmp * o_cmp[:, cs] + g_slc * o_slc[:, cs] + g_win * o_win[:, cs])

    o = ot_ref[...].T.astype(BF16)
    y_ref[0] = x_ref[0] + _dot(o, wout_ref[...])


def _nsa_prompt(qt, qrt, gt, ck, cvt, kslc, vslct, kwin, vwint, ovt, x, w_out):
    b, t, d = x.shape
    nqb = t // QBLOCK
    ns = ovt.shape[0]
    whole = lambda a: pl.BlockSpec((1,) + a.shape[1:], lambda i, j, nd=a.ndim: (i,) + (0,) * (nd - 1))
    return pl.pallas_call(
        _nsa_prompt_body,
        grid=(b, nqb),
        in_specs=[
            pl.BlockSpec((1, 1, D_MODEL, QBLOCK), lambda i, j: (i, j, 0, 0)),
            pl.BlockSpec((1, 1, D_MODEL, QBLOCK), lambda i, j: (i, j, 0, 0)),
            pl.BlockSpec((1, 1, 3 * N_HEADS, QBLOCK), lambda i, j: (i, j, 0, 0)),
            whole(ck), whole(cvt), whole(kslc), whole(vslct), whole(kwin), whole(vwint),
            pl.BlockSpec(ovt.shape, lambda i, j: (0, 0)),
            pl.BlockSpec((1, QBLOCK, d), lambda i, j: (i, j, 0)),
            pl.BlockSpec(w_out.shape, lambda i, j: (0, 0)),
        ],
        out_specs=pl.BlockSpec((1, QBLOCK, d), lambda i, j: (i, j, 0)),
        out_shape=jax.ShapeDtypeStruct((b, t, d), F32),
        scratch_shapes=[pltpu.VMEM((D_MODEL, QBLOCK), F32),
                        pltpu.VMEM((ns, QBLOCK), F32),
                        pltpu.VMEM((ns, HPG * QBLOCK), F32)],
        compiler_params=_params(("arbitrary", "arbitrary")),
        name="nsa_prompt_attn",
    )(qt, qrt, gt, ck, cvt, kslc, vslct, kwin, vwint, ovt, x, w_out)


def _softmax_rows(s_past, valid_past, s_new, valid_new):
    s_past = jnp.where(valid_past, s_past, NEG_INF)
    s_new = jnp.where(valid_new, s_new, NEG_INF)
    m = jnp.maximum(jnp.max(s_past, axis=1, keepdims=True), s_new)
    p = jnp.exp(s_past - m)
    p_new = jnp.exp(s_new - m)
    l = jnp.sum(p, axis=1, keepdims=True) + p_new
    return p, p_new, l


def _nsa_sample_body(pt_ref, cache_ref, cwin_ref, qbd_ref, qrbd_ref, kvn_ref, wnn_ref, gates_ref,
                     pe_ref, wk_ref, wv_ref, ov_ref, exp_ref, o_ref, buf_ref, sem_ref,
                     *, past_len, n_pages):
    n = pl.program_id(0)
    nseq = pl.num_programs(0)
    slot = n % 2

    def page_copy(seq, p, sl):
        return pltpu.make_async_copy(cache_ref.at[pt_ref[seq, p]], buf_ref.at[sl, p], sem_ref.at[sl])

    @pl.when(n == 0)
    def _():
        for p in range(n_pages):
            page_copy(0, p, 0).start()

    @pl.when(n + 1 < nseq)
    def _():
        for p in range(n_pages):
            page_copy(n + 1, p, 1 - slot).start()

    for p in range(n_pages):
        page_copy(n, p, slot).wait()

    chunks = 4 * KV_W // LANES

    def page_cols(kind, first, count, step):
        cols = []
        for c in range(KV_W // LANES):
            start = first * chunks + kind * (KV_W // LANES) + c
            part = buf_ref[slot, :, pl.ds(start, count, stride=step * chunks), :]
            cols.append(part.reshape(n_pages * count, LANES))
        return jnp.concatenate(cols, axis=1)

    nseg = past_len // CMP_STRIDE
    segs_per_page = PAGE_SIZE // CMP_STRIDE
    kvn = kvn_ref[0]
    wnn = wnn_ref[0]
    row8 = lax.broadcasted_iota(jnp.int32, (SUBLANES, KV_W), 0)

    cmp = []
    for kind, w_ref in ((0, wk_ref), (1, wv_ref)):
        lo = jnp.zeros((nseg + SUBLANES, KV_W), F32)
        hi = jnp.zeros((nseg + SUBLANES, KV_W), F32)
        new_row = kvn[:, kind * KV_W:(kind + 1) * KV_W]
        for l in range(CMP_STRIDE):
            xs = page_cols(kind, l, segs_per_page, CMP_STRIDE)
            tail = jnp.where(row8 == 0, new_row, 0.0) if l == 0 else jnp.zeros((SUBLANES, KV_W), F32)
            xs = jnp.concatenate([xs, tail], axis=0)
            lo = lo + _dot((xs + pe_ref[kind, l:l + 1, :]).astype(BF16), w_ref[0, l])
            hi = hi + _dot((xs + pe_ref[kind, CMP_STRIDE + l:CMP_STRIDE + l + 1, :]).astype(BF16), w_ref[1, l])
        blocks = lo + pltpu.roll(hi, nseg + SUBLANES - 1, axis=0)
        cmp.append(blocks[0:nseg].astype(BF16))
    ck, cv = cmp
    nc = nseg
    qpos = past_len

    qbd = qbd_ref[0]
    qrbd = qrbd_ref[0]
    c_idx = lax.broadcasted_iota(jnp.int32, (N_HEADS, nc), 1)
    cvalid = c_idx * CMP_STRIDE + (CMP_BLOCK - 1) <= qpos
    sm = jnp.where(cvalid, _dot_nt(qbd, ck), NEG_INF)
    e = jnp.where(cvalid, jnp.exp(sm - jnp.max(sm, axis=1, keepdims=True)), 0.0)
    den = jnp.sum(e, axis=1, keepdims=True)
    p_cmp = (e / jnp.where(den > 0.0, den, 1.0)).astype(BF16)
    o_cmp = _dot(p_cmp, cv)
    imp16 = _dot(p_cmp, ov_ref[...])

    nsl = ov_ref.shape[1]
    j_row = lax.broadcasted_iota(jnp.int32, (nsl, nsl), 1)
    j_col = lax.broadcasted_iota(jnp.int32, (nsl, nsl), 0)
    j1 = lax.broadcasted_iota(jnp.int32, (1, nsl), 1)
    cur = qpos // SEL_BLOCK
    forced = (j1 == 0) | (j1 == cur) | (j1 == cur - 1)
    svalid = j1 * SEL_BLOCK <= qpos
    sel_rows = []
    for g in range(N_KV):
        imp = (imp16[g * HPG:g * HPG + 1] + imp16[g * HPG + 1:g * HPG + 2]
               + imp16[g * HPG + 2:g * HPG + 3] + imp16[g * HPG + 3:g * HPG + 4])
        score = jnp.where(svalid, imp + jnp.where(forced, FORCE_BONUS, 0.0), NEG_INF)
        s_b = jnp.broadcast_to(score, (nsl, nsl))
        s_t = jnp.sum(jnp.where(j_row == j_col, s_b, 0.0), axis=1, keepdims=True)
        beats = (s_t > s_b) | ((s_t == s_b) & (j_col < j_row))
        rank = jnp.sum(jnp.where(beats, 1.0, 0.0), axis=0, keepdims=True)
        sel = jnp.where(rank < TOP_N, 1.0, 0.0)
        sel_rows.append(jnp.broadcast_to(sel, (HPG, nsl)))
    sel16 = jnp.concatenate(sel_rows, axis=0)
    key_sel = _dot(sel16.astype(BF16), exp_ref[...]) > 0.5
    new_sel = sel16[:, cur:cur + 1] > 0.5

    def new_key_score(k_new):
        return jnp.sum(qrbd.astype(F32) * k_new.astype(BF16).astype(F32), axis=1, keepdims=True)

    def new_val(p_new, v_new):
        return p_new.astype(BF16).astype(F32) * v_new.astype(BF16).astype(F32)

    k_slc = page_cols(2, 0, PAGE_SIZE, 1).astype(BF16)
    v_slc = page_cols(3, 0, PAGE_SIZE, 1).astype(BF16)
    p, p_new, l = _softmax_rows(_dot_nt(qrbd, k_slc), key_sel,
                                new_key_score(kvn[:, 2 * KV_W:3 * KV_W]), new_sel)
    o_slc = (_dot(p.astype(BF16), v_slc) + new_val(p_new, kvn[:, 3 * KV_W:4 * KV_W])) / l

    wb = cwin_ref.shape[1]
    k_win = cwin_ref[0, :, 0:KV_W].astype(BF16)
    v_win = cwin_ref[0, :, KV_W:2 * KV_W].astype(BF16)
    wpos = past_len - wb + lax.broadcasted_iota(jnp.int32, (N_HEADS, wb), 1)
    dist = qpos - wpos
    wvalid = (wpos >= 0) & (dist >= 0) & (dist < WINDOW)
    p, p_new, l = _softmax_rows(_dot_nt(qrbd, k_win), wvalid,
                                new_key_score(wnn[:, 0:KV_W]), jnp.full((N_HEADS, 1), True))
    o_win = (_dot(p.astype(BF16), v_win) + new_val(p_new, wnn[:, KV_W:2 * KV_W])) / l

    o = gates_ref[0, 0] * o_cmp + gates_ref[0, 1] * o_slc + gates_ref[0, 2] * o_win
    hrow = lax.broadcasted_iota(jnp.int32, (N_HEADS, KV_W), 0)
    gcol = lax.broadcasted_iota(jnp.int32, (N_HEADS, KV_W), 1)
    o_ref[0] = jnp.where(hrow // HPG == gcol // HEAD_DIM, o, 0.0)


def _nsa_sample(page_table, cache, cwin, qbd, qrbd, kvn, wnn, gates, pe_t, wk_bd, wv_bd, ov, expand):
    nseq, n_pages = page_table.shape
    past_len = n_pages * PAGE_SIZE
    blk = lambda a: pl.BlockSpec((1,) + a.shape[1:], lambda i, pt, nd=a.ndim: (i,) + (0,) * (nd - 1))
    full = lambda a: pl.BlockSpec(a.shape, lambda i, pt, nd=a.ndim: (0,) * nd)
    grid_spec = pltpu.PrefetchScalarGridSpec(
        num_scalar_prefetch=1,
        grid=(nseq,),
        in_specs=[
            pl.BlockSpec(memory_space=pl.ANY),
            blk(cwin), blk(qbd), blk(qrbd), blk(kvn), blk(wnn), blk(gates),
            full(pe_t), full(wk_bd), full(wv_bd), full(ov), full(expand),
        ],
        out_specs=pl.BlockSpec((1, N_HEADS, KV_W), lambda i, pt: (i, 0, 0)),
        scratch_shapes=[pltpu.VMEM((2, n_pages) + cache.shape[1:], F32),
                        pltpu.SemaphoreType.DMA((2,))],
    )
    return pl.pallas_call(
        functools.partial(_nsa_sample_body, past_len=past_len, n_pages=n_pages),
        grid_spec=grid_spec,
        out_shape=jax.ShapeDtypeStruct((nseq, N_HEADS, KV_W), F32),
        compiler_params=_params(("arbitrary",)),
        name="nsa_sample_attn",
    )(page_table, cache, cwin, qbd, qrbd, kvn, wnn, gates, pe_t, wk_bd, wv_bd, ov, expand)


def _out_proj_body(o_ref, w_ref, x_ref, y_ref):
    y_ref[...] = x_ref[...] + _dot(o_ref[...].astype(BF16), w_ref[...])


def _out_proj(o, w, x):
    args = (o, w, x)
    return pl.pallas_call(
        _out_proj_body,
        grid=(1,),
        in_specs=[pl.BlockSpec(a.shape, lambda i: (0, 0)) for a in args],
        out_specs=pl.BlockSpec(x.shape, lambda i: (0, 0)),
        out_shape=jax.ShapeDtypeStruct(x.shape, F32),
        compiler_params=_params(("arbitrary",)),
        name="out_proj_sample",
    )(*args)


def _rope_tables(pos):
    half = HEAD_DIM // 2
    inv = ROPE_THETA ** (-jnp.arange(half, dtype=F32) * 2.0 / HEAD_DIM)
    ang = pos.astype(F32)[:, None] * inv[None, :]
    cos = jnp.concatenate([jnp.cos(ang), jnp.cos(ang)], -1)
    sin_signed = jnp.concatenate([-jnp.sin(ang), jnp.sin(ang)], -1)
    reps = LANES // HEAD_DIM
    return jnp.tile(cos, (1, reps)), jnp.tile(sin_signed, (1, reps))


def _block_diag_cmp(w):
    eye = jnp.eye(N_KV, dtype=w.dtype)
    bd = jnp.einsum('gh,lde->lgdhe', eye, w).reshape(CMP_BLOCK, KV_W, KV_W)
    return bd.reshape(2, CMP_STRIDE, KV_W, KV_W).astype(BF16)


def _overlap(nc, ns):
    cstart = np.arange(nc) * CMP_STRIDE
    sstart = np.arange(ns) * SEL_BLOCK
    ov = ((cstart[:, None] < sstart[None, :] + SEL_BLOCK)
          & (cstart[:, None] + CMP_BLOCK > sstart[None, :]))
    return ov.astype(np.float32)


def _row(v):
    return v.reshape(1, -1)


def _ffn_weights(w_ffn_in, w_ffn_out):
    return w_ffn_in.astype(BF16), w_ffn_out.astype(BF16)


def _ffn_rows(x, g, ffn_w, layer, g_final, final_norm):
    shape = x.shape
    x2 = x.reshape(-1, shape[-1])
    tr = min(512, x2.shape[0])
    y = _ffn(x2, _row(g), ffn_w[0][layer], ffn_w[1][layer], _row(g_final), final_norm, tr=tr)
    return y.reshape(shape)


def _attn_weights(w_in1, pe_cmp, w_cmp, w_out1):
    d = w_in1.shape[0]
    q_dim = N_HEADS * HEAD_DIM
    kv_dim = 6 * KV_W
    own = (jnp.arange(N_HEADS)[:, None] // HPG == jnp.arange(N_KV)[None, :]).astype(F32)
    w_out1_exp = (w_out1.reshape(N_HEADS, 1, HEAD_DIM, d) * own[:, :, None, None]).reshape(N_HEADS * KV_W, d)
    return dict(
        w_qkv=w_in1[:, :q_dim + kv_dim].astype(BF16),
        w_gate=jnp.zeros((d, LANES), F32).at[:, :3 * N_HEADS].set(w_in1[:, q_dim + kv_dim:]).astype(BF16),
        w_out1=w_out1.astype(BF16),
        w_out1_exp=w_out1_exp.astype(BF16),
        wk_bd=_block_diag_cmp(w_cmp[0]),
        wv_bd=_block_diag_cmp(w_cmp[1]),
        pe_t=jnp.tile(pe_cmp, (1, 1, N_KV)),
        own=own,
    )


def _layer0_mixer(x_prompt, xs, state_conv, g, w_in0, conv_w, norm_v, w_spatial, b_spatial, w_out0):
    w_in0_b = w_in0.astype(BF16)
    w_out0_b = w_out0.astype(BF16)
    cw8 = jnp.zeros((SUBLANES, D_CONV), F32).at[0:conv_w.shape[0]].set(conv_w)
    causal = jnp.tril(jnp.ones((CHUNK, CHUNK), dtype=bool))
    wsp = jnp.where(causal[None], w_spatial, 0).astype(BF16)
    per_head = D_CHUNK // N_CHUNK_HEADS
    bsp = jnp.repeat(b_spatial.T, per_head, axis=1)
    w00 = _row(jnp.repeat(w_spatial[:, 0, 0], per_head))
    b0 = _row(jnp.repeat(b_spatial[:, 0], per_head))
    tr = min(512, x_prompt.shape[1])
    xp, ztail = _l0_prompt(x_prompt, _row(g), w_in0_b, cw8, _row(norm_v), wsp, bsp, w_out0_b, tr=tr)
    conv_state_prompt = ztail[:, SUBLANES - (conv_w.shape[0] - 1):]
    xs, z_s, vn_s = _l0_sample(xs, state_conv[:, 0], state_conv[:, 1], _row(g), w_in0_b, cw8,
                               _row(norm_v), w00, b0, w_out0_b)
    conv_state_sample = jnp.concatenate([state_conv[:, 1:], z_s[:, None]], axis=1)
    return xp, xs, conv_state_prompt, conv_state_sample, vn_s[:, None]


def _layer1_prompt(xp, g, aw):
    b, t, d = xp.shape
    cos_p, sin_p = _rope_tables(jnp.arange(t))
    tr = min(512, t)
    qt, qrt, gt, kv_p, win_p, kslc, kwin, vslct, vwint = _l1_proj_prompt(
        xp, _row(g), aw["w_qkv"], aw["w_gate"], cos_p, sin_p, tr=tr)
    ck, cvt = _compress_prompt(kv_p, aw["pe_t"], aw["wk_bd"], aw["wv_bd"])
    nseg = t // CMP_STRIDE
    ns = -(-t // SEL_BLOCK)
    ovt = jnp.asarray(_overlap(nseg, ns).T, dtype=BF16)
    xp = _nsa_prompt(qt, qrt, gt, ck, cvt, kslc, vslct, kwin, vwint, ovt, xp, aw["w_out1"])
    kv_prompt = kv_p.reshape(b, t, 4, N_KV, HEAD_DIM)
    wlen = min(WINDOW, t)
    win_prompt = win_p[:, t - wlen:].reshape(b, wlen, 2, N_KV, HEAD_DIM)
    return xp, kv_prompt, win_prompt


def _layer1_sample(xs, cache_kv, cache_win, page_table, g, aw):
    n, d = xs.shape
    past_len = page_table.shape[1] * PAGE_SIZE
    cos_s, sin_s = _rope_tables(jnp.full((n,), past_len))
    q_s, qr_s, gates_s, kv_s, win_s = _l1_proj_sample(xs, _row(g), aw["w_qkv"], aw["w_gate"], cos_s, sin_s)
    own = aw["own"]

    def group_expand(q):
        qh = q.reshape(n, N_HEADS, 1, HEAD_DIM) * own[None, :, :, None]
        return qh.reshape(n, N_HEADS, KV_W).astype(BF16)

    gates_l = gates_s[:, :3 * N_HEADS].reshape(n, N_HEADS, 3).transpose(0, 2, 1)[..., None]
    nc_s = past_len // CMP_STRIDE
    ns_s = -(-(past_len + 1) // SEL_BLOCK)
    ov_s = np.zeros((nc_s, LANES), np.float32)
    ov_s[:, :ns_s] = _overlap(nc_s, ns_s)
    expand = (np.arange(LANES)[:, None] == (np.arange(past_len) // SEL_BLOCK)[None, :]).astype(np.float32)
    o_bd = _nsa_sample(
        page_table, cache_kv.reshape(cache_kv.shape[0], PAGE_SIZE * 4 * KV_W // LANES, LANES),
        cache_win.reshape(n, cache_win.shape[1], 2 * KV_W),
        group_expand(q_s), group_expand(qr_s), kv_s[:, None], win_s[:, None], gates_l,
        aw["pe_t"], aw["wk_bd"], aw["wv_bd"], jnp.asarray(ov_s, dtype=BF16), jnp.asarray(expand, dtype=BF16))
    xs = _out_proj(o_bd.reshape(n, N_HEADS * KV_W), aw["w_out1_exp"], xs)
    return xs, kv_s.reshape(n, 1, 4, N_KV, HEAD_DIM), win_s.reshape(n, 1, 2, N_KV, HEAD_DIM)


def kernel(x_prompt, x_sample, state_conv, cache_kv, cache_win, page_table, norm_mix, norm_ffn, norm_final,
           w_in0, conv_w, norm_v, w_spatial, b_spatial, w_out0, w_in1, pe_cmp, w_cmp, w_out1,
           w_ffn_in, w_ffn_out):
    b, t, d = x_prompt.shape
    n = x_sample.shape[0]
    xs = x_sample.reshape(n, d)
    ffn_w = _ffn_weights(w_ffn_in, w_ffn_out)
    attn_w = _attn_weights(w_in1, pe_cmp, w_cmp, w_out1)

    xp, xs, conv_state_prompt, conv_state_sample, chunk_v_sample = _layer0_mixer(
        x_prompt, xs, state_conv, norm_mix[0], w_in0, conv_w, norm_v, w_spatial, b_spatial, w_out0)
    xp = _ffn_rows(xp, norm_ffn[0], ffn_w, 0, norm_final, False)
    xs = _ffn_rows(xs, norm_ffn[0], ffn_w, 0, norm_final, False)

    xp, kv_prompt, win_prompt = _layer1_prompt(xp, norm_mix[1], attn_w)
    xs, kv_sample, win_sample = _layer1_sample(xs, cache_kv, cache_win, page_table, norm_mix[1], attn_w)
    y_prompt = _ffn_rows(xp, norm_ffn[1], ffn_w, 1, norm_final, True)
    y_sample = _ffn_rows(xs, norm_ffn[1], ffn_w, 1, norm_final, True)
    return (y_prompt, y_sample.reshape(n, 1, d), conv_state_prompt, conv_state_sample, chunk_v_sample,
            kv_prompt, kv_sample, win_prompt, win_sample)
```

```python
import functools

import numpy as np
import jax
import jax.numpy as jnp
from jax import lax
from jax.experimental import pallas as pl
from jax.experimental.pallas import tpu as pltpu

F32 = jnp.float32
BF16 = jnp.bfloat16

EPS = 1e-6
D_MODEL = 1024
D_CONV = 512
D_CHUNK = 512
N_CHUNK_HEADS = 4
CHUNK = 128
HEAD_DIM = 64
N_HEADS = 16
N_KV = 4
HPG = 4
KV_W = N_KV * HEAD_DIM
CMP_BLOCK = 32
CMP_STRIDE = 16
SEL_BLOCK = 64
TOP_N = 16
WINDOW = 512
QBLOCK = 128
SLC_KEYS = 256
PAGE_SIZE = 128
ROPE_THETA = 10000.0
SCALE = HEAD_DIM ** -0.5
NEG_INF = -1e30
FORCE_BONUS = 1e4
D_FF = 2816

LANES = 128
SUBLANES = 8
VMEM_LIMIT = 56 * 1024 * 1024


def _params(sem):
    return pltpu.CompilerParams(dimension_semantics=sem, vmem_limit_bytes=VMEM_LIMIT)


def _rms(x, g):
    ms = jnp.mean(x * x, axis=-1, keepdims=True)
    return x * lax.rsqrt(ms + EPS) * g


def _dot(a, b):
    return jnp.dot(a, b, preferred_element_type=F32)


def _dot_nt(a, b):
    return lax.dot_general(a, b, (((1,), (1,)), ((), ())), preferred_element_type=F32)


def _l0_prompt_body(x_ref, g_ref, win_ref, cw_ref, nv_ref, wsp_ref, bsp_ref, wout_ref,
                    y_ref, ztail_ref, carry_ref, ymix_ref):
    t = pl.program_id(1)
    x = x_ref[0]
    tr = x.shape[0]
    h = _dot(_rms(x, g_ref[...]).astype(BF16), win_ref[...])
    b_gate = h[:, 0:D_CONV]
    c_gate = h[:, D_CONV:2 * D_CONV]
    h_conv = h[:, 2 * D_CONV:3 * D_CONV]
    u = h[:, 3 * D_CONV:3 * D_CONV + D_CHUNK]
    v = h[:, 3 * D_CONV + D_CHUNK:]
    z = c_gate * h_conv

    @pl.when(t == 0)
    def _():
        carry_ref[...] = jnp.zeros_like(carry_ref)

    prev = carry_ref[...]
    row = lax.broadcasted_iota(jnp.int32, z.shape, 0)
    z1 = jnp.where(row == 0, prev[7:8], pltpu.roll(z, 1, axis=0))
    z2 = jnp.where(row == 0, prev[6:7], jnp.where(row == 1, prev[7:8], pltpu.roll(z, 2, axis=0)))
    cw = cw_ref[...]
    conv = cw[0:1] * z2 + cw[1:2] * z1 + cw[2:3] * z
    carry_ref[...] = z[tr - SUBLANES:]
    ztail_ref[0] = z[tr - SUBLANES:]
    ymix_ref[:, 0:D_CONV] = (b_gate * conv).astype(BF16)

    vn = _rms(jax.nn.gelu(v), nv_ref[...]).astype(BF16)
    gu = jax.nn.gelu(u)
    for c in range(tr // CHUNK):
        rs = slice(c * CHUNK, (c + 1) * CHUNK)
        for hd in range(N_CHUNK_HEADS):
            cs = slice(hd * LANES, (hd + 1) * LANES)
            mixed = _dot(wsp_ref[hd], vn[rs, cs]) + bsp_ref[:, cs]
            ymix_ref[rs, D_CONV + hd * LANES:D_CONV + (hd + 1) * LANES] = (gu[rs, cs] * mixed).astype(BF16)
    y_ref[0] = x + _dot(ymix_ref[...], wout_ref[...])


def _l0_prompt(x, g, w_in0, cw8, nv, wsp, bsp, w_out0, tr=512):
    b, t, d = x.shape
    const = lambda *shape: pl.BlockSpec(shape, lambda i, j: (0,) * len(shape))
    return pl.pallas_call(
        _l0_prompt_body,
        grid=(b, t // tr),
        in_specs=[
            pl.BlockSpec((1, tr, d), lambda i, j: (i, j, 0)),
            const(1, d), const(*w_in0.shape), const(*cw8.shape), const(1, D_CHUNK),
            const(*wsp.shape), const(*bsp.shape), const(*w_out0.shape),
        ],
        out_specs=[
            pl.BlockSpec((1, tr, d), lambda i, j: (i, j, 0)),
            pl.BlockSpec((1, SUBLANES, D_CONV), lambda i, j: (i, 0, 0)),
        ],
        out_shape=[jax.ShapeDtypeStruct((b, t, d), F32),
                   jax.ShapeDtypeStruct((b, SUBLANES, D_CONV), F32)],
        scratch_shapes=[pltpu.VMEM((SUBLANES, D_CONV), F32), pltpu.VMEM((tr, D_MODEL), BF16)],
        compiler_params=_params(("arbitrary", "arbitrary")),
        name="l0_mixer_prompt",
    )(x, g, w_in0, cw8, nv, wsp, bsp, w_out0)


def _l0_sample_body(x_ref, h0_ref, h1_ref, g_ref, win_ref, cw_ref, nv_ref, w00_ref, b0_ref, wout_ref,
                    y_ref, z_ref, vn_ref):
    x = x_ref[...]
    h = _dot(_rms(x, g_ref[...]).astype(BF16), win_ref[...])
    b_gate = h[:, 0:D_CONV]
    c_gate = h[:, D_CONV:2 * D_CONV]
    h_conv = h[:, 2 * D_CONV:3 * D_CONV]
    u = h[:, 3 * D_CONV:3 * D_CONV + D_CHUNK]
    v = h[:, 3 * D_CONV + D_CHUNK:]
    z = c_gate * h_conv
    cw = cw_ref[...]
    conv = cw[0:1] * h0_ref[...] + cw[1:2] * h1_ref[...] + cw[2:3] * z
    vn = _rms(jax.nn.gelu(v), nv_ref[...])
    mixed = w00_ref[...] * vn + b0_ref[...]
    ymix = jnp.concatenate([b_gate * conv, jax.nn.gelu(u) * mixed], axis=1).astype(BF16)
    y_ref[...] = x + _dot(ymix, wout_ref[...])
    z_ref[...] = z
    vn_ref[...] = vn


def _l0_sample(x, h0, h1, g, w_in0, cw8, nv, w00, b0, w_out0):
    n, d = x.shape
    args = (x, h0, h1, g, w_in0, cw8, nv, w00, b0, w_out0)
    return pl.pallas_call(
        _l0_sample_body,
        grid=(1,),
        in_specs=[pl.BlockSpec(a.shape, lambda i, nd=a.ndim: (0,) * nd) for a in args],
        out_specs=[pl.BlockSpec((n, d), lambda i: (0, 0)),
                   pl.BlockSpec((n, D_CONV), lambda i: (0, 0)),
                   pl.BlockSpec((n, D_CHUNK), lambda i: (0, 0))],
        out_shape=[jax.ShapeDtypeStruct((n, d), F32),
                   jax.ShapeDtypeStruct((n, D_CONV), F32),
                   jax.ShapeDtypeStruct((n, D_CHUNK), F32)],
        compiler_params=_params(("arbitrary",)),
        name="l0_mixer_sample",
    )(*args)


def _ffn_body(x_ref, g_ref, wa_ref, wb_ref, wo_ref, gf_ref, o_ref, xn_ref, acc_ref, *, final_norm):
    j = pl.program_id(1)

    @pl.when(j == 0)
    def _():
        x = x_ref[...]
        xn_ref[...] = _rms(x, g_ref[...]).astype(BF16)
        acc_ref[...] = x

    xn = xn_ref[...]
    a = _dot(xn, wa_ref[...])
    b = _dot(xn, wb_ref[...])
    act = (jax.nn.silu(a) * b).astype(BF16)
    acc_ref[...] += _dot(act, wo_ref[...])

    @pl.when(j == pl.num_programs(1) - 1)
    def _():
        r = acc_ref[...]
        o_ref[...] = _rms(r, gf_ref[...]) if final_norm else r


def _ffn(x, g, w_in, w_out, gf, final_norm, tr, nf=2):
    rows, d = x.shape
    tf = D_FF // nf
    return pl.pallas_call(
        functools.partial(_ffn_body, final_norm=final_norm),
        grid=(rows // tr, nf),
        in_specs=[
            pl.BlockSpec((tr, d), lambda i, j: (i, 0)),
            pl.BlockSpec((1, d), lambda i, j: (0, 0)),
            pl.BlockSpec((d, tf), lambda i, j: (0, j)),
            pl.BlockSpec((d, tf), lambda i, j: (0, nf + j)),
            pl.BlockSpec((tf, d), lambda i, j: (j, 0)),
            pl.BlockSpec((1, d), lambda i, j: (0, 0)),
        ],
        out_specs=pl.BlockSpec((tr, d), lambda i, j: (i, 0)),
        out_shape=jax.ShapeDtypeStruct((rows, d), F32),
        scratch_shapes=[pltpu.VMEM((tr, d), BF16), pltpu.VMEM((tr, d), F32)],
        compiler_params=_params(("arbitrary", "arbitrary")),
        name="ffn_final" if final_norm else "ffn",
    )(x, g, w_in, w_in, w_out, gf)


def _rope_cols(xs, cos, sin_signed, first_half):
    outs = []
    for c in range(xs.shape[1] // LANES):
        ch = xs[:, c * LANES:(c + 1) * LANES]
        rot = jnp.where(first_half, pltpu.roll(ch, LANES - HEAD_DIM // 2, axis=1),
                        pltpu.roll(ch, HEAD_DIM // 2, axis=1))
        outs.append(ch * cos + rot * sin_signed)
    return outs[0] if len(outs) == 1 else jnp.concatenate(outs, axis=1)


def _l1_proj_body(x_ref, g_ref, w_ref, wg_ref, cos_ref, sin_ref, *outs, prompt):
    x = x_ref[0] if prompt else x_ref[...]
    tr = x.shape[0]
    xn = _rms(x, g_ref[...]).astype(BF16)
    h = _dot(xn, w_ref[...])
    gates = jax.nn.sigmoid(_dot(xn, wg_ref[...]))
    cos = cos_ref[...]
    sin_signed = sin_ref[...]
    lane = lax.broadcasted_iota(jnp.int32, cos.shape, 1)
    first_half = (lane % HEAD_DIM) < (HEAD_DIM // 2)
    rope = functools.partial(_rope_cols, cos=cos, sin_signed=sin_signed, first_half=first_half)

    q = h[:, 0:D_MODEL] * SCALE
    q_rot = rope(q)
    o = D_MODEL
    k_cmp_v_cmp = h[:, o:o + 2 * KV_W]
    k_slc = rope(h[:, o + 2 * KV_W:o + 3 * KV_W])
    v_slc = h[:, o + 3 * KV_W:o + 4 * KV_W]
    k_win = rope(h[:, o + 4 * KV_W:o + 5 * KV_W])
    v_win = h[:, o + 5 * KV_W:o + 6 * KV_W]

    if not prompt:
        q_ref, qr_ref, g_out, kv_ref, wn_ref = outs
        q_ref[...] = q
        qr_ref[...] = q_rot
        g_out[...] = gates
        kv_ref[:, 0:2 * KV_W] = k_cmp_v_cmp
        kv_ref[:, 2 * KV_W:3 * KV_W] = k_slc
        kv_ref[:, 3 * KV_W:4 * KV_W] = v_slc
        wn_ref[:, 0:KV_W] = k_win
        wn_ref[:, KV_W:2 * KV_W] = v_win
        return

    qt_ref, qrt_ref, gt_ref, kv_ref, wn_ref, kslc_ref, kwin_ref, vslct_ref, vwint_ref = outs
    kv_ref[0, :, 0:2 * KV_W] = k_cmp_v_cmp
    kv_ref[0, :, 2 * KV_W:3 * KV_W] = k_slc
    kv_ref[0, :, 3 * KV_W:4 * KV_W] = v_slc
    wn_ref[0, :, 0:KV_W] = k_win
    wn_ref[0, :, KV_W:2 * KV_W] = v_win
    qt = q.T.astype(BF16)
    qrt = q_rot.T.astype(BF16)
    gt = gates.T
    vst = v_slc.T.astype(BF16)
    vwt = v_win.T.astype(BF16)
    for r in range(tr // QBLOCK):
        cs = slice(r * QBLOCK, (r + 1) * QBLOCK)
        qt_ref[0, r] = qt[:, cs]
        qrt_ref[0, r] = qrt[:, cs]
        gt_ref[0, r] = gt[0:3 * N_HEADS, cs]
        for gi in range(N_KV):
            vslct_ref[0, gi, r] = vst[gi * HEAD_DIM:(gi + 1) * HEAD_DIM, cs]
            vwint_ref[0, gi, r] = vwt[gi * HEAD_DIM:(gi + 1) * HEAD_DIM, cs]
    for gi in range(N_KV):
        kslc_ref[0, gi] = k_slc[:, gi * HEAD_DIM:(gi + 1) * HEAD_DIM].astype(BF16)
        kwin_ref[0, gi] = k_win[:, gi * HEAD_DIM:(gi + 1) * HEAD_DIM].astype(BF16)


def _l1_proj_prompt(x, g, w_qkv, w_gate, cos, sin_signed, tr=512):
    b, t, d = x.shape
    nqb = t // QBLOCK
    rb = tr // QBLOCK
    const = lambda *shape: pl.BlockSpec(shape, lambda i, j: (0,) * len(shape))
    out_shape = [
        jax.ShapeDtypeStruct((b, nqb, D_MODEL, QBLOCK), BF16),
        jax.ShapeDtypeStruct((b, nqb, D_MODEL, QBLOCK), BF16),
        jax.ShapeDtypeStruct((b, nqb, 3 * N_HEADS, QBLOCK), F32),
        jax.ShapeDtypeStruct((b, t, 4 * KV_W), F32),
        jax.ShapeDtypeStruct((b, t, 2 * KV_W), F32),
        jax.ShapeDtypeStruct((b, N_KV, t, HEAD_DIM), BF16),
        jax.ShapeDtypeStruct((b, N_KV, t, HEAD_DIM), BF16),
        jax.ShapeDtypeStruct((b, N_KV, nqb, HEAD_DIM, QBLOCK), BF16),
        jax.ShapeDtypeStruct((b, N_KV, nqb, HEAD_DIM, QBLOCK), BF16),
    ]
    out_specs = [
        pl.BlockSpec((1, rb, D_MODEL, QBLOCK), lambda i, j: (i, j, 0, 0)),
        pl.BlockSpec((1, rb, D_MODEL, QBLOCK), lambda i, j: (i, j, 0, 0)),
        pl.BlockSpec((1, rb, 3 * N_HEADS, QBLOCK), lambda i, j: (i, j, 0, 0)),
        pl.BlockSpec((1, tr, 4 * KV_W), lambda i, j: (i, j, 0)),
        pl.BlockSpec((1, tr, 2 * KV_W), lambda i, j: (i, j, 0)),
        pl.BlockSpec((1, N_KV, tr, HEAD_DIM), lambda i, j: (i, 0, j, 0)),
        pl.BlockSpec((1, N_KV, tr, HEAD_DIM), lambda i, j: (i, 0, j, 0)),
        pl.BlockSpec((1, N_KV, rb, HEAD_DIM, QBLOCK), lambda i, j: (i, 0, j, 0, 0)),
        pl.BlockSpec((1, N_KV, rb, HEAD_DIM, QBLOCK), lambda i, j: (i, 0, j, 0, 0)),
    ]
    return pl.pallas_call(
        functools.partial(_l1_proj_body, prompt=True),
        grid=(b, t // tr),
        in_specs=[
            pl.BlockSpec((1, tr, d), lambda i, j: (i, j, 0)),
            const(1, d), const(*w_qkv.shape), const(*w_gate.shape),
            pl.BlockSpec((tr, LANES), lambda i, j: (j, 0)),
            pl.BlockSpec((tr, LANES), lambda i, j: (j, 0)),
        ],
        out_specs=out_specs,
        out_shape=out_shape,
        compiler_params=_params(("arbitrary", "arbitrary")),
        name="l1_proj_prompt",
    )(x, g, w_qkv, w_gate, cos, sin_signed)


def _l1_proj_sample(x, g, w_qkv, w_gate, cos, sin_signed):
    n, d = x.shape
    args = (x, g, w_qkv, w_gate, cos, sin_signed)
    widths = (D_MODEL, D_MODEL, LANES, 4 * KV_W, 2 * KV_W)
    return pl.pallas_call(
        functools.partial(_l1_proj_body, prompt=False),
        grid=(1,),
        in_specs=[pl.BlockSpec(a.shape, lambda i, nd=a.ndim: (0,) * nd) for a in args],
        out_specs=[pl.BlockSpec((n, w), lambda i: (0, 0)) for w in widths],
        out_shape=[jax.ShapeDtypeStruct((n, w), F32) for w in widths],
        compiler_params=_params(("arbitrary",)),
        name="l1_proj_sample",
    )(*args)


def _compress_prompt_body(x_ref, pe_ref, wk_ref, wv_ref, ck_ref, cvt_ref, acc_ref):
    l = pl.program_id(1)
    x = x_ref[0]
    parts = []
    for kind, w_ref in ((0, wk_ref), (1, wv_ref)):
        xs = x[:, kind * KV_W:(kind + 1) * KV_W]
        for half in range(2):
            pe = pe_ref[kind, pl.ds(half * CMP_STRIDE + l, 1), :]
            parts.append(_dot((xs + pe).astype(BF16), w_ref[half, 0]))

    @pl.when(l == 0)
    def _():
        for i, p in enumerate(parts):
            acc_ref[i] = p

    @pl.when(l > 0)
    def _():
        for i, p in enumerate(parts):
            acc_ref[i] += p

    @pl.when(l == pl.num_programs(1) - 1)
    def _():
        nseg = x.shape[0]
        ck = acc_ref[0] + pltpu.roll(acc_ref[1], nseg - 1, axis=0)
        cv = acc_ref[2] + pltpu.roll(acc_ref[3], nseg - 1, axis=0)
        cvt = cv.T
        for gi in range(N_KV):
            ck_ref[0, gi] = ck[:, gi * HEAD_DIM:(gi + 1) * HEAD_DIM].astype(BF16)
            cvt_ref[0, gi] = cvt[gi * HEAD_DIM:(gi + 1) * HEAD_DIM, :].astype(BF16)


def _compress_prompt(kv, pe_t, wk_bd, wv_bd):
    b, t, _ = kv.shape
    nseg = t // CMP_STRIDE
    kvs = kv.reshape(b, nseg, CMP_STRIDE * 4 * KV_W)
    return pl.pallas_call(
        _compress_prompt_body,
        grid=(b, CMP_STRIDE),
        in_specs=[
            pl.BlockSpec((1, nseg, 2 * KV_W), lambda i, l: (i, 0, 2 * l)),
            pl.BlockSpec(pe_t.shape, lambda i, l: (0, 0, 0)),
            pl.BlockSpec((2, 1, KV_W, KV_W), lambda i, l: (0, l, 0, 0)),
            pl.BlockSpec((2, 1, KV_W, KV_W), lambda i, l: (0, l, 0, 0)),
        ],
        out_specs=[
            pl.BlockSpec((1, N_KV, nseg, HEAD_DIM), lambda i, l: (i, 0, 0, 0)),
            pl.BlockSpec((1, N_KV, HEAD_DIM, nseg), lambda i, l: (i, 0, 0, 0)),
        ],
        out_shape=[jax.ShapeDtypeStruct((b, N_KV, nseg, HEAD_DIM), BF16),
                   jax.ShapeDtypeStruct((b, N_KV, HEAD_DIM, nseg), BF16)],
        scratch_shapes=[pltpu.VMEM((4, nseg, KV_W), F32)],
        compiler_params=_params(("arbitrary", "arbitrary")),
        name="compress_prompt",
    )(kvs, pe_t, wk_bd, wv_bd)


def _online_block(s, masks, vt, carry):
    m, l, acc = carry
    for mask in masks:
        s = jnp.where(mask, s, NEG_INF)
    m_new = jnp.maximum(m, jnp.max(s, axis=0, keepdims=True))
    alpha = jnp.exp(m - m_new)
    p = jnp.exp(s - m_new)
    l = alpha * l + jnp.sum(p, axis=0, keepdims=True)
    acc = alpha * acc + _dot(vt, p.astype(BF16))
    return m_new, l, acc


def _nsa_prompt_body(qt_ref, qrt_ref, gt_ref, ck_ref, cvt_ref, kslc_ref, vslct_ref, kwin_ref, vwint_ref,
                     ovt_ref, x_ref, wout_ref, y_ref, ot_ref, score_ref, sel_ref, m_ref, l_ref, acc_ref):
    i = pl.program_id(1)
    qs = i * QBLOCK
    nlane = HPG * QBLOCK
    nc = ck_ref.shape[2]
    ns = ovt_ref.shape[0]

    def head_cols(ref, g):
        return jnp.concatenate(
            [ref[0, 0, (g * HPG + h) * HEAD_DIM:(g * HPG + h + 1) * HEAD_DIM, :] for h in range(HPG)], axis=1)

    def gate_rows(g, branch):
        return jnp.concatenate(
            [gt_ref[0, 0, 3 * (g * HPG + h) + branch:3 * (g * HPG + h) + branch + 1, :] for h in range(HPG)], axis=1)

    def put_out(g, val, first):
        for h in range(HPG):
            rs = slice((g * HPG + h) * HEAD_DIM, (g * HPG + h + 1) * HEAD_DIM)
            piece = val[:, h * QBLOCK:(h + 1) * QBLOCK]
            ot_ref[rs, :] = piece if first else ot_ref[rs, :] + piece

    for g in range(N_KV):
        q_t = head_cols(qt_ref, g)

        sc = _dot(ck_ref[0, g], q_t)
        c_idx = lax.broadcasted_iota(jnp.int32, (nc, nlane), 0)
        qpos = qs + lax.broadcasted_iota(jnp.int32, (nc, nlane), 1) % QBLOCK
        cvalid = c_idx * CMP_STRIDE + (CMP_BLOCK - 1) <= qpos
        sm = jnp.where(cvalid, sc, NEG_INF)
        e = jnp.where(cvalid, jnp.exp(sm - jnp.max(sm, axis=0, keepdims=True)), 0.0)
        den = jnp.sum(e, axis=0, keepdims=True)
        p_cmp = (e / jnp.where(den > 0.0, den, 1.0)).astype(BF16)
        o_cmp = _dot(cvt_ref[0, g], p_cmp)
        imp4 = _dot(ovt_ref[...], p_cmp)
        imp = (imp4[:, 0:QBLOCK] + imp4[:, QBLOCK:2 * QBLOCK]
               + imp4[:, 2 * QBLOCK:3 * QBLOCK] + imp4[:, 3 * QBLOCK:4 * QBLOCK])

        j_idx = lax.broadcasted_iota(jnp.int32, (ns, QBLOCK), 0)
        qp = qs + lax.broadcasted_iota(jnp.int32, (ns, QBLOCK), 1)
        cur = qp // SEL_BLOCK
        forced = (j_idx == 0) | (j_idx == cur) | (j_idx == cur - 1)
        svalid = j_idx * SEL_BLOCK <= qp
        score = jnp.where(svalid, imp + jnp.where(forced, FORCE_BONUS, 0.0), NEG_INF)
        score_ref[...] = score

        def rank_step(jp, cnt):
            rb = jnp.broadcast_to(score_ref[pl.ds(jp, 1), :], (ns, QBLOCK))
            tie = jnp.where(jp < j_idx, 1, 0)
            return cnt + jnp.where(rb > score, 1, jnp.where(rb == score, tie, 0))

        n_cand = jnp.minimum(2 * i + 2, ns)
        cnt = lax.fori_loop(0, n_cand, rank_step, jnp.zeros((ns, QBLOCK), jnp.int32))
        sel_ref[g] = jnp.concatenate([jnp.where(cnt < TOP_N, 1.0, 0.0)] * HPG, axis=1)
        put_out(g, gate_rows(g, 0) * o_cmp, True)

    kbs = SLC_KEYS // QBLOCK
    sbs = SLC_KEYS // SEL_BLOCK
    m_ref[...] = jnp.full(m_ref.shape, NEG_INF, F32)
    l_ref[...] = jnp.zeros(l_ref.shape, F32)
    acc_ref[...] = jnp.zeros(acc_ref.shape, F32)

    def slc_block(kb, extra):
        k0 = pl.multiple_of(kb * SLC_KEYS, SLC_KEYS)
        for g in range(N_KV):
            s = _dot(kslc_ref[0, g, pl.ds(k0, SLC_KEYS), :], head_cols(qrt_ref, g))
            sel = jnp.concatenate(
                [jnp.broadcast_to(sel_ref[g, pl.ds(sbs * kb + j, 1), :], (SEL_BLOCK, nlane)) for j in range(sbs)],
                axis=0)
            vt = jnp.concatenate([vslct_ref[0, g, kbs * kb + j] for j in range(kbs)], axis=1)
            m, l, acc = _online_block(s, [sel > 0.5] + extra, vt, (m_ref[g], l_ref[g], acc_ref[g]))
            m_ref[g] = m
            l_ref[g] = l
            acc_ref[g] = acc

    last = qs // SLC_KEYS
    key_rel = (lax.broadcasted_iota(jnp.int32, (SLC_KEYS, nlane), 0)
               - lax.broadcasted_iota(jnp.int32, (SLC_KEYS, nlane), 1) % QBLOCK)
    slc_block(last, [key_rel <= qs - last * SLC_KEYS])

    def slc_loop(kb, c):
        slc_block(kb, [])
        return c

    lax.fori_loop(0, last, slc_loop, 0)
    for g in range(N_KV):
        put_out(g, gate_rows(g, 1) * (acc_ref[g] / l_ref[g]), False)

    nwb = WINDOW // QBLOCK + 1
    kr = lax.broadcasted_iota(jnp.int32, (QBLOCK, nlane), 0)
    qc = lax.broadcasted_iota(jnp.int32, (QBLOCK, nlane), 1) % QBLOCK
    for g in range(N_KV):
        qr_t = head_cols(qrt_ref, g)
        s_parts, v_parts = [], []
        for j in range(nwb):
            kb = i - (nwb - 1) + j
            kbc = jnp.maximum(kb, 0)
            s = _dot(kwin_ref[0, g, pl.ds(pl.multiple_of(kbc * QBLOCK, QBLOCK), QBLOCK), :], qr_t)
            if j == 0:
                s = jnp.where(kr > qc, s, NEG_INF)
            if j == nwb - 1:
                s = jnp.where(kr <= qc, s, NEG_INF)
            else:
                s = jnp.where(kb >= 0, s, NEG_INF)
            s_parts.append(s)
            v_parts.append(vwint_ref[0, g, kbc])
        s = jnp.concatenate(s_parts, axis=0)
        p = jnp.exp(s - jnp.max(s, axis=0, keepdims=True))
        o_win = _dot(jnp.concatenate(v_parts, axis=1), p.astype(BF16)) / jnp.sum(p, axis=0, keepdims=True)
        put_out(g, gate_rows(g, 2) * o_win, False)

    o = ot_ref[...].T.astype(BF16)
    y_ref[0] = x_ref[0] + _dot(o, wout_ref[...])


def _nsa_prompt(qt, qrt, gt, ck, cvt, kslc, vslct, kwin, vwint, ovt, x, w_out):
    b, t, d = x.shape
    nqb = t // QBLOCK
    ns = ovt.shape[0]
    whole = lambda a: pl.BlockSpec((1,) + a.shape[1:], lambda i, j, nd=a.ndim: (i,) + (0,) * (nd - 1))
    return pl.pallas_call(
        _nsa_prompt_body,
        grid=(b, nqb),
        in_specs=[
            pl.BlockSpec((1, 1, D_MODEL, QBLOCK), lambda i, j: (i, j, 0, 0)),
            pl.BlockSpec((1, 1, D_MODEL, QBLOCK), lambda i, j: (i, j, 0, 0)),
            pl.BlockSpec((1, 1, 3 * N_HEADS, QBLOCK), lambda i, j: (i, j, 0, 0)),
            whole(ck), whole(cvt), whole(kslc), whole(vslct), whole(kwin), whole(vwint),
            pl.BlockSpec(ovt.shape, lambda i, j: (0, 0)),
            pl.BlockSpec((1, QBLOCK, d), lambda i, j: (i, j, 0)),
            pl.BlockSpec(w_out.shape, lambda i, j: (0, 0)),
        ],
        out_specs=pl.BlockSpec((1, QBLOCK, d), lambda i, j: (i, j, 0)),
        out_shape=jax.ShapeDtypeStruct((b, t, d), F32),
        scratch_shapes=[pltpu.VMEM((D_MODEL, QBLOCK), F32),
                        pltpu.VMEM((ns, QBLOCK), F32),
                        pltpu.VMEM((N_KV, ns, HPG * QBLOCK), F32),
                        pltpu.VMEM((N_KV, 1, HPG * QBLOCK), F32),
                        pltpu.VMEM((N_KV, 1, HPG * QBLOCK), F32),
                        pltpu.VMEM((N_KV, HEAD_DIM, HPG * QBLOCK), F32)],
        compiler_params=_params(("arbitrary", "arbitrary")),
        name="nsa_prompt_attn",
    )(qt, qrt, gt, ck, cvt, kslc, vslct, kwin, vwint, ovt, x, w_out)


def _softmax_rows(s_past, valid_past, s_new, valid_new):
    s_past = jnp.where(valid_past, s_past, NEG_INF)
    s_new = jnp.where(valid_new, s_new, NEG_INF)
    m = jnp.maximum(jnp.max(s_past, axis=1, keepdims=True), s_new)
    p = jnp.exp(s_past - m)
    p_new = jnp.exp(s_new - m)
    l = jnp.sum(p, axis=1, keepdims=True) + p_new
    return p, p_new, l


def _nsa_sample_body(pt_ref, cache_ref, cwin_ref, qbd_ref, qrbd_ref, kvn_ref, wnn_ref, gates_ref,
                     pe_ref, wk_ref, wv_ref, ov_ref, exp_ref, o_ref, buf_ref, xs_ref, sem_ref,
                     *, past_len, n_pages):
    n = pl.program_id(0)
    nseq = pl.num_programs(0)
    slot = n % 2

    def page_copy(seq, p, sl):
        return pltpu.make_async_copy(cache_ref.at[pt_ref[seq, p]], buf_ref.at[sl, p], sem_ref.at[sl])

    @pl.when(n == 0)
    def _():
        for p in range(n_pages):
            page_copy(0, p, 0).start()

    @pl.when(n + 1 < nseq)
    def _():
        for p in range(n_pages):
            page_copy(n + 1, p, 1 - slot).start()

    for p in range(n_pages):
        page_copy(n, p, slot).wait()

    def cached_t(kind):
        return jnp.concatenate(
            [buf_ref[slot, p, pl.ds(kind * KV_W, KV_W), :] for p in range(n_pages)], axis=1).astype(BF16)

    nseg = past_len // CMP_STRIDE
    halves = KV_W // LANES

    for kind in range(2):
        for p in range(n_pages):
            for c in range(halves):
                xt = buf_ref[slot, p, pl.ds(kind * KV_W + c * LANES, LANES), :]
                xs_ref[kind, c, p * PAGE_SIZE:(p + 1) * PAGE_SIZE, :] = xt.T
    kvn = kvn_ref[0]
    wnn = wnn_ref[0]
    row8 = lax.broadcasted_iota(jnp.int32, (SUBLANES, KV_W), 0)

    cmp = []
    for kind, w_ref in ((0, wk_ref), (1, wv_ref)):
        lo = jnp.zeros((nseg + SUBLANES, KV_W), F32)
        hi = jnp.zeros((nseg + SUBLANES, KV_W), F32)
        new_row = kvn[:, kind * KV_W:(kind + 1) * KV_W]
        for l in range(CMP_STRIDE):
            xs = jnp.concatenate(
                [xs_ref[kind, c, pl.ds(l, nseg, stride=CMP_STRIDE), :] for c in range(halves)], axis=1)
            tail = jnp.where(row8 == 0, new_row, 0.0) if l == 0 else jnp.zeros((SUBLANES, KV_W), F32)
            xs = jnp.concatenate([xs, tail], axis=0)
            lo = lo + _dot((xs + pe_ref[kind, l:l + 1, :]).astype(BF16), w_ref[0, l])
            hi = hi + _dot((xs + pe_ref[kind, CMP_STRIDE + l:CMP_STRIDE + l + 1, :]).astype(BF16), w_ref[1, l])
        blocks = lo + pltpu.roll(hi, nseg + SUBLANES - 1, axis=0)
        cmp.append(blocks[0:nseg].astype(BF16))
    ck, cv = cmp
    nc = nseg
    qpos = past_len

    qbd = qbd_ref[0]
    qrbd = qrbd_ref[0]
    c_idx = lax.broadcasted_iota(jnp.int32, (N_HEADS, nc), 1)
    cvalid = c_idx * CMP_STRIDE + (CMP_BLOCK - 1) <= qpos
    sm = jnp.where(cvalid, _dot_nt(qbd, ck), NEG_INF)
    e = jnp.where(cvalid, jnp.exp(sm - jnp.max(sm, axis=1, keepdims=True)), 0.0)
    den = jnp.sum(e, axis=1, keepdims=True)
    p_cmp = (e / jnp.where(den > 0.0, den, 1.0)).astype(BF16)
    o_cmp = _dot(p_cmp, cv)
    imp16 = _dot(p_cmp, ov_ref[...])

    nsl = ov_ref.shape[1]
    j_row = lax.broadcasted_iota(jnp.int32, (nsl, nsl), 1)
    j_col = lax.broadcasted_iota(jnp.int32, (nsl, nsl), 0)
    j1 = lax.broadcasted_iota(jnp.int32, (1, nsl), 1)
    cur = qpos // SEL_BLOCK
    forced = (j1 == 0) | (j1 == cur) | (j1 == cur - 1)
    svalid = j1 * SEL_BLOCK <= qpos
    sel_rows = []
    for g in range(N_KV):
        imp = (imp16[g * HPG:g * HPG + 1] + imp16[g * HPG + 1:g * HPG + 2]
               + imp16[g * HPG + 2:g * HPG + 3] + imp16[g * HPG + 3:g * HPG + 4])
        score = jnp.where(svalid, imp + jnp.where(forced, FORCE_BONUS, 0.0), NEG_INF)
        s_b = jnp.broadcast_to(score, (nsl, nsl))
        s_t = jnp.sum(jnp.where(j_row == j_col, s_b, 0.0), axis=1, keepdims=True)
        beats = (s_t > s_b) | ((s_t == s_b) & (j_col < j_row))
        rank = jnp.sum(jnp.where(beats, 1.0, 0.0), axis=0, keepdims=True)
        sel = jnp.where(rank < TOP_N, 1.0, 0.0)
        sel_rows.append(jnp.broadcast_to(sel, (HPG, nsl)))
    sel16 = jnp.concatenate(sel_rows, axis=0)
    key_sel = _dot(sel16.astype(BF16), exp_ref[...]) > 0.5
    new_sel = sel16[:, cur:cur + 1] > 0.5

    def new_key_score(k_new):
        return jnp.sum(qrbd.astype(F32) * k_new.astype(BF16).astype(F32), axis=1, keepdims=True)

    def new_val(p_new, v_new):
        return p_new.astype(BF16).astype(F32) * v_new.astype(BF16).astype(F32)

    p, p_new, l = _softmax_rows(_dot(qrbd, cached_t(2)), key_sel,
                                new_key_score(kvn[:, 2 * KV_W:3 * KV_W]), new_sel)
    o_slc = (_dot_nt(p.astype(BF16), cached_t(3)) + new_val(p_new, kvn[:, 3 * KV_W:4 * KV_W])) / l

    wb = cwin_ref.shape[2]
    k_win_t = cwin_ref[0, 0:KV_W, :].astype(BF16)
    v_win_t = cwin_ref[0, KV_W:2 * KV_W, :].astype(BF16)
    wpos = past_len - wb + lax.broadcasted_iota(jnp.int32, (N_HEADS, wb), 1)
    dist = qpos - wpos
    wvalid = (wpos >= 0) & (dist >= 0) & (dist < WINDOW)
    p, p_new, l = _softmax_rows(_dot(qrbd, k_win_t), wvalid,
                                new_key_score(wnn[:, 0:KV_W]), jnp.full((N_HEADS, 1), True))
    o_win = (_dot_nt(p.astype(BF16), v_win_t) + new_val(p_new, wnn[:, KV_W:2 * KV_W])) / l

    o = gates_ref[0, 0] * o_cmp + gates_ref[0, 1] * o_slc + gates_ref[0, 2] * o_win
    hrow = lax.broadcasted_iota(jnp.int32, (N_HEADS, KV_W), 0)
    gcol = lax.broadcasted_iota(jnp.int32, (N_HEADS, KV_W), 1)
    o_ref[0] = jnp.where(hrow // HPG == gcol // HEAD_DIM, o, 0.0)


def _nsa_sample(page_table, cache, cwin, qbd, qrbd, kvn, wnn, gates, pe_t, wk_bd, wv_bd, ov, expand):
    nseq, n_pages = page_table.shape
    past_len = n_pages * PAGE_SIZE
    blk = lambda a: pl.BlockSpec((1,) + a.shape[1:], lambda i, pt, nd=a.ndim: (i,) + (0,) * (nd - 1))
    full = lambda a: pl.BlockSpec(a.shape, lambda i, pt, nd=a.ndim: (0,) * nd)
    grid_spec = pltpu.PrefetchScalarGridSpec(
        num_scalar_prefetch=1,
        grid=(nseq,),
        in_specs=[
            pl.BlockSpec(memory_space=pl.ANY),
            blk(cwin), blk(qbd), blk(qrbd), blk(kvn), blk(wnn), blk(gates),
            full(pe_t), full(wk_bd), full(wv_bd), full(ov), full(expand),
        ],
        out_specs=pl.BlockSpec((1, N_HEADS, KV_W), lambda i, pt: (i, 0, 0)),
        scratch_shapes=[pltpu.VMEM((2, n_pages) + cache.shape[1:], F32),
                        pltpu.VMEM((2, KV_W // LANES, past_len, LANES), F32),
                        pltpu.SemaphoreType.DMA((2,))],
    )
    return pl.pallas_call(
        functools.partial(_nsa_sample_body, past_len=past_len, n_pages=n_pages),
        grid_spec=grid_spec,
        out_shape=jax.ShapeDtypeStruct((nseq, N_HEADS, KV_W), F32),
        compiler_params=_params(("arbitrary",)),
        name="nsa_sample_attn",
    )(page_table, cache, cwin, qbd, qrbd, kvn, wnn, gates, pe_t, wk_bd, wv_bd, ov, expand)


def _out_proj_body(o_ref, w_ref, x_ref, y_ref):
    y_ref[...] = x_ref[...] + _dot(o_ref[...].astype(BF16), w_ref[...])


def _out_proj(o, w, x):
    args = (o, w, x)
    return pl.pallas_call(
        _out_proj_body,
        grid=(1,),
        in_specs=[pl.BlockSpec(a.shape, lambda i: (0, 0)) for a in args],
        out_specs=pl.BlockSpec(x.shape, lambda i: (0, 0)),
        out_shape=jax.ShapeDtypeStruct(x.shape, F32),
        compiler_params=_params(("arbitrary",)),
        name="out_proj_sample",
    )(*args)


def _rope_tables(pos):
    half = HEAD_DIM // 2
    inv = ROPE_THETA ** (-jnp.arange(half, dtype=F32) * 2.0 / HEAD_DIM)
    ang = pos.astype(F32)[:, None] * inv[None, :]
    cos = jnp.concatenate([jnp.cos(ang), jnp.cos(ang)], -1)
    sin_signed = jnp.concatenate([-jnp.sin(ang), jnp.sin(ang)], -1)
    reps = LANES // HEAD_DIM
    return jnp.tile(cos, (1, reps)), jnp.tile(sin_signed, (1, reps))


def _block_diag_cmp(w):
    eye = jnp.eye(N_KV, dtype=w.dtype)
    bd = jnp.einsum('gh,lde->lgdhe', eye, w).reshape(CMP_BLOCK, KV_W, KV_W)
    return bd.reshape(2, CMP_STRIDE, KV_W, KV_W).astype(BF16)


def _overlap(nc, ns):
    cstart = np.arange(nc) * CMP_STRIDE
    sstart = np.arange(ns) * SEL_BLOCK
    ov = ((cstart[:, None] < sstart[None, :] + SEL_BLOCK)
          & (cstart[:, None] + CMP_BLOCK > sstart[None, :]))
    return ov.astype(np.float32)


def _row(v):
    return v.reshape(1, -1)


def _ffn_weights(w_ffn_in, w_ffn_out):
    return w_ffn_in.astype(BF16), w_ffn_out.astype(BF16)


def _ffn_rows(x, g, ffn_w, layer, g_final, final_norm):
    shape = x.shape
    x2 = x.reshape(-1, shape[-1])
    tr = min(512, x2.shape[0])
    y = _ffn(x2, _row(g), ffn_w[0][layer], ffn_w[1][layer], _row(g_final), final_norm, tr=tr)
    return y.reshape(shape)


def _attn_weights(w_in1, pe_cmp, w_cmp, w_out1):
    d = w_in1.shape[0]
    q_dim = N_HEADS * HEAD_DIM
    kv_dim = 6 * KV_W
    own = (jnp.arange(N_HEADS)[:, None] // HPG == jnp.arange(N_KV)[None, :]).astype(F32)
    w_out1_exp = (w_out1.reshape(N_HEADS, 1, HEAD_DIM, d) * own[:, :, None, None]).reshape(N_HEADS * KV_W, d)
    return dict(
        w_qkv=w_in1[:, :q_dim + kv_dim].astype(BF16),
        w_gate=jnp.zeros((d, LANES), F32).at[:, :3 * N_HEADS].set(w_in1[:, q_dim + kv_dim:]).astype(BF16),
        w_out1=w_out1.astype(BF16),
        w_out1_exp=w_out1_exp.astype(BF16),
        wk_bd=_block_diag_cmp(w_cmp[0]),
        wv_bd=_block_diag_cmp(w_cmp[1]),
        pe_t=jnp.tile(pe_cmp, (1, 1, N_KV)),
        own=own,
    )


def _layer0_mixer(x_prompt, xs, state_conv, g, w_in0, conv_w, norm_v, w_spatial, b_spatial, w_out0):
    w_in0_b = w_in0.astype(BF16)
    w_out0_b = w_out0.astype(BF16)
    cw8 = jnp.zeros((SUBLANES, D_CONV), F32).at[0:conv_w.shape[0]].set(conv_w)
    causal = jnp.tril(jnp.ones((CHUNK, CHUNK), dtype=bool))
    wsp = jnp.where(causal[None], w_spatial, 0).astype(BF16)
    per_head = D_CHUNK // N_CHUNK_HEADS
    bsp = jnp.repeat(b_spatial.T, per_head, axis=1)
    w00 = _row(jnp.repeat(w_spatial[:, 0, 0], per_head))
    b0 = _row(jnp.repeat(b_spatial[:, 0], per_head))
    tr = min(512, x_prompt.shape[1])
    xp, ztail = _l0_prompt(x_prompt, _row(g), w_in0_b, cw8, _row(norm_v), wsp, bsp, w_out0_b, tr=tr)
    conv_state_prompt = ztail[:, SUBLANES - (conv_w.shape[0] - 1):]
    xs, z_s, vn_s = _l0_sample(xs, state_conv[:, 0], state_conv[:, 1], _row(g), w_in0_b, cw8,
                               _row(norm_v), w00, b0, w_out0_b)
    conv_state_sample = jnp.concatenate([state_conv[:, 1:], z_s[:, None]], axis=1)
    return xp, xs, conv_state_prompt, conv_state_sample, vn_s[:, None]


def _layer1_prompt(xp, g, aw):
    b, t, d = xp.shape
    cos_p, sin_p = _rope_tables(jnp.arange(t))
    tr = min(512, t)
    qt, qrt, gt, kv_p, win_p, kslc, kwin, vslct, vwint = _l1_proj_prompt(
        xp, _row(g), aw["w_qkv"], aw["w_gate"], cos_p, sin_p, tr=tr)
    ck, cvt = _compress_prompt(kv_p, aw["pe_t"], aw["wk_bd"], aw["wv_bd"])
    nseg = t // CMP_STRIDE
    ns = -(-t // SEL_BLOCK)
    ovt = jnp.asarray(_overlap(nseg, ns).T, dtype=BF16)
    xp = _nsa_prompt(qt, qrt, gt, ck, cvt, kslc, vslct, kwin, vwint, ovt, xp, aw["w_out1"])
    kv_prompt = kv_p.reshape(b, t, 4, N_KV, HEAD_DIM)
    wlen = min(WINDOW, t)
    win_prompt = win_p[:, t - wlen:].reshape(b, wlen, 2, N_KV, HEAD_DIM)
    return xp, kv_prompt, win_prompt


def _layer1_sample(xs, cache_kv, cache_win, page_table, g, aw):
    n, d = xs.shape
    past_len = page_table.shape[1] * PAGE_SIZE
    cos_s, sin_s = _rope_tables(jnp.full((n,), past_len))
    q_s, qr_s, gates_s, kv_s, win_s = _l1_proj_sample(xs, _row(g), aw["w_qkv"], aw["w_gate"], cos_s, sin_s)
    own = aw["own"]

    def group_expand(q):
        qh = q.reshape(n, N_HEADS, 1, HEAD_DIM) * own[None, :, :, None]
        return qh.reshape(n, N_HEADS, KV_W).astype(BF16)

    gates_l = gates_s[:, :3 * N_HEADS].reshape(n, N_HEADS, 3).transpose(0, 2, 1)[..., None]
    nc_s = past_len // CMP_STRIDE
    ns_s = -(-(past_len + 1) // SEL_BLOCK)
    ov_s = np.zeros((nc_s, LANES), np.float32)
    ov_s[:, :ns_s] = _overlap(nc_s, ns_s)
    expand = (np.arange(LANES)[:, None] == (np.arange(past_len) // SEL_BLOCK)[None, :]).astype(np.float32)
    o_bd = _nsa_sample(
        page_table, cache_kv.transpose(0, 2, 3, 4, 1).reshape(cache_kv.shape[0], 4 * KV_W, PAGE_SIZE),
        cache_win.transpose(0, 2, 3, 4, 1).reshape(n, 2 * KV_W, cache_win.shape[1]),
        group_expand(q_s), group_expand(qr_s), kv_s[:, None], win_s[:, None], gates_l,
        aw["pe_t"], aw["wk_bd"], aw["wv_bd"], jnp.asarray(ov_s, dtype=BF16), jnp.asarray(expand, dtype=BF16))
    xs = _out_proj(o_bd.reshape(n, N_HEADS * KV_W), aw["w_out1_exp"], xs)
    return xs, kv_s.reshape(n, 1, 4, N_KV, HEAD_DIM), win_s.reshape(n, 1, 2, N_KV, HEAD_DIM)


def kernel(x_prompt, x_sample, state_conv, cache_kv, cache_win, page_table, norm_mix, norm_ffn, norm_final,
           w_in0, conv_w, norm_v, w_spatial, b_spatial, w_out0, w_in1, pe_cmp, w_cmp, w_out1,
           w_ffn_in, w_ffn_out):
    b, t, d = x_prompt.shape
    n = x_sample.shape[0]
    xs = x_sample.reshape(n, d)
    ffn_w = _ffn_weights(w_ffn_in, w_ffn_out)
    attn_w = _attn_weights(w_in1, pe_cmp, w_cmp, w_out1)

    xp, xs, conv_state_prompt, conv_state_sample, chunk_v_sample = _layer0_mixer(
        x_prompt, xs, state_conv, norm_mix[0], w_in0, conv_w, norm_v, w_spatial, b_spatial, w_out0)
    xp = _ffn_rows(xp, norm_ffn[0], ffn_w, 0, norm_final, False)
    xs = _ffn_rows(xs, norm_ffn[0], ffn_w, 0, norm_final, False)

    xp, kv_prompt, win_prompt = _layer1_prompt(xp, norm_mix[1], attn_w)
    xs, kv_sample, win_sample = _layer1_sample(xs, cache_kv, cache_win, page_table, norm_mix[1], attn_w)
    y_prompt = _ffn_rows(xp, norm_ffn[1], ffn_w, 1, norm_final, True)
    y_sample = _ffn_rows(xs, norm_ffn[1], ffn_w, 1, norm_final, True)
    return (y_prompt, y_sample.reshape(n, 1, d), conv_state_prompt, conv_state_sample, chunk_v_sample,
            kv_prompt, kv_sample, win_prompt, win_sample)
```

```python
import functools
import math

import numpy as np
import jax
import jax.numpy as jnp
from jax import lax
from jax.experimental import pallas as pl
from jax.experimental.pallas import tpu as pltpu

F32 = jnp.float32
BF16 = jnp.bfloat16

EPS = 1e-6
D_MODEL = 1024
D_CONV = 512
D_CHUNK = 512
N_CHUNK_HEADS = 4
CHUNK = 128
HEAD_DIM = 64
N_HEADS = 16
N_KV = 4
HPG = 4
KV_W = N_KV * HEAD_DIM
CMP_BLOCK = 32
CMP_STRIDE = 16
SEL_BLOCK = 64
TOP_N = 16
WINDOW = 512
QBLOCK = 128
SLC_KEYS = 512
SLC_SBLK = SLC_KEYS // SEL_BLOCK
SLC_PAR = 2
KX_W = 128
VX_ROWS = 80
PAGE_SIZE = 128
ROPE_THETA = 10000.0
SCALE = HEAD_DIM ** -0.5
Q_SCALE = SCALE * math.log2(math.e)
NEG_INF = -1e30
FORCE_BONUS = 1e4
D_FF = 2816

LANES = 128
SUBLANES = 8
VMEM_LIMIT = 56 * 1024 * 1024


def _params(sem):
    return pltpu.CompilerParams(dimension_semantics=sem, vmem_limit_bytes=VMEM_LIMIT)


def _rms(x, g):
    ms = jnp.mean(x * x, axis=-1, keepdims=True)
    return x * lax.rsqrt(ms + EPS) * g


def _dot(a, b):
    return jnp.dot(a, b, preferred_element_type=F32)


def _dot_nt(a, b):
    return lax.dot_general(a, b, (((1,), (1,)), ((), ())), preferred_element_type=F32)


def _l0_prompt_body(x_ref, g_ref, win_ref, cw_ref, nv_ref, wsp_ref, bsp_ref, wout_ref,
                    y_ref, ztail_ref, carry_ref, ymix_ref):
    t = pl.program_id(1)
    x = x_ref[0]
    tr = x.shape[0]
    h = _dot(_rms(x, g_ref[...]).astype(BF16), win_ref[...])
    b_gate = h[:, 0:D_CONV]
    c_gate = h[:, D_CONV:2 * D_CONV]
    h_conv = h[:, 2 * D_CONV:3 * D_CONV]
    u = h[:, 3 * D_CONV:3 * D_CONV + D_CHUNK]
    v = h[:, 3 * D_CONV + D_CHUNK:]
    z = c_gate * h_conv

    @pl.when(t == 0)
    def _():
        carry_ref[...] = jnp.zeros_like(carry_ref)

    prev = carry_ref[...]
    row = lax.broadcasted_iota(jnp.int32, z.shape, 0)
    z1 = jnp.where(row == 0, prev[7:8], pltpu.roll(z, 1, axis=0))
    z2 = jnp.where(row == 0, prev[6:7], jnp.where(row == 1, prev[7:8], pltpu.roll(z, 2, axis=0)))
    cw = cw_ref[...]
    conv = cw[0:1] * z2 + cw[1:2] * z1 + cw[2:3] * z
    carry_ref[...] = z[tr - SUBLANES:]
    ztail_ref[0] = z[tr - SUBLANES:]
    ymix_ref[:, 0:D_CONV] = (b_gate * conv).astype(BF16)

    vn = _rms(jax.nn.gelu(v), nv_ref[...]).astype(BF16)
    gu = jax.nn.gelu(u)
    for c in range(tr // CHUNK):
        rs = slice(c * CHUNK, (c + 1) * CHUNK)
        for hd in range(N_CHUNK_HEADS):
            cs = slice(hd * LANES, (hd + 1) * LANES)
            mixed = _dot(wsp_ref[hd], vn[rs, cs]) + bsp_ref[:, cs]
            ymix_ref[rs, D_CONV + hd * LANES:D_CONV + (hd + 1) * LANES] = (gu[rs, cs] * mixed).astype(BF16)
    y_ref[0] = x + _dot(ymix_ref[...], wout_ref[...])


def _l0_prompt(x, g, w_in0, cw8, nv, wsp, bsp, w_out0, tr=512):
    b, t, d = x.shape
    const = lambda *shape: pl.BlockSpec(shape, lambda i, j: (0,) * len(shape))
    return pl.pallas_call(
        _l0_prompt_body,
        grid=(b, t // tr),
        in_specs=[
            pl.BlockSpec((1, tr, d), lambda i, j: (i, j, 0)),
            const(1, d), const(*w_in0.shape), const(*cw8.shape), const(1, D_CHUNK),
            const(*wsp.shape), const(*bsp.shape), const(*w_out0.shape),
        ],
        out_specs=[
            pl.BlockSpec((1, tr, d), lambda i, j: (i, j, 0)),
            pl.BlockSpec((1, SUBLANES, D_CONV), lambda i, j: (i, 0, 0)),
        ],
        out_shape=[jax.ShapeDtypeStruct((b, t, d), F32),
                   jax.ShapeDtypeStruct((b, SUBLANES, D_CONV), F32)],
        scratch_shapes=[pltpu.VMEM((SUBLANES, D_CONV), F32), pltpu.VMEM((tr, D_MODEL), BF16)],
        compiler_params=_params(("arbitrary", "arbitrary")),
        name="l0_mixer_prompt",
    )(x, g, w_in0, cw8, nv, wsp, bsp, w_out0)


def _l0_sample_body(x_ref, h0_ref, h1_ref, g_ref, win_ref, cw_ref, nv_ref, w00_ref, b0_ref, wout_ref,
                    y_ref, z_ref, vn_ref):
    x = x_ref[...]
    h = _dot(_rms(x, g_ref[...]).astype(BF16), win_ref[...])
    b_gate = h[:, 0:D_CONV]
    c_gate = h[:, D_CONV:2 * D_CONV]
    h_conv = h[:, 2 * D_CONV:3 * D_CONV]
    u = h[:, 3 * D_CONV:3 * D_CONV + D_CHUNK]
    v = h[:, 3 * D_CONV + D_CHUNK:]
    z = c_gate * h_conv
    cw = cw_ref[...]
    conv = cw[0:1] * h0_ref[...] + cw[1:2] * h1_ref[...] + cw[2:3] * z
    vn = _rms(jax.nn.gelu(v), nv_ref[...])
    mixed = w00_ref[...] * vn + b0_ref[...]
    ymix = jnp.concatenate([b_gate * conv, jax.nn.gelu(u) * mixed], axis=1).astype(BF16)
    y_ref[...] = x + _dot(ymix, wout_ref[...])
    z_ref[...] = z
    vn_ref[...] = vn


def _l0_sample(x, h0, h1, g, w_in0, cw8, nv, w00, b0, w_out0):
    n, d = x.shape
    args = (x, h0, h1, g, w_in0, cw8, nv, w00, b0, w_out0)
    return pl.pallas_call(
        _l0_sample_body,
        grid=(1,),
        in_specs=[pl.BlockSpec(a.shape, lambda i, nd=a.ndim: (0,) * nd) for a in args],
        out_specs=[pl.BlockSpec((n, d), lambda i: (0, 0)),
                   pl.BlockSpec((n, D_CONV), lambda i: (0, 0)),
                   pl.BlockSpec((n, D_CHUNK), lambda i: (0, 0))],
        out_shape=[jax.ShapeDtypeStruct((n, d), F32),
                   jax.ShapeDtypeStruct((n, D_CONV), F32),
                   jax.ShapeDtypeStruct((n, D_CHUNK), F32)],
        compiler_params=_params(("arbitrary",)),
        name="l0_mixer_sample",
    )(*args)


def _ffn_body(x_ref, g_ref, wa_ref, wb_ref, wo_ref, gf_ref, o_ref, xn_ref, acc_ref, *, final_norm):
    j = pl.program_id(1)

    @pl.when(j == 0)
    def _():
        x = x_ref[...]
        xn_ref[...] = _rms(x, g_ref[...]).astype(BF16)
        acc_ref[...] = x

    xn = xn_ref[...]
    a = _dot(xn, wa_ref[...])
    b = _dot(xn, wb_ref[...])
    act = (jax.nn.silu(a) * b).astype(BF16)
    acc_ref[...] += _dot(act, wo_ref[...])

    @pl.when(j == pl.num_programs(1) - 1)
    def _():
        r = acc_ref[...]
        o_ref[...] = _rms(r, gf_ref[...]) if final_norm else r


def _ffn(x, g, w_in, w_out, gf, final_norm, tr, nf=2):
    rows, d = x.shape
    tf = D_FF // nf
    return pl.pallas_call(
        functools.partial(_ffn_body, final_norm=final_norm),
        grid=(rows // tr, nf),
        in_specs=[
            pl.BlockSpec((tr, d), lambda i, j: (i, 0)),
            pl.BlockSpec((1, d), lambda i, j: (0, 0)),
            pl.BlockSpec((d, tf), lambda i, j: (0, j)),
            pl.BlockSpec((d, tf), lambda i, j: (0, nf + j)),
            pl.BlockSpec((tf, d), lambda i, j: (j, 0)),
            pl.BlockSpec((1, d), lambda i, j: (0, 0)),
        ],
        out_specs=pl.BlockSpec((tr, d), lambda i, j: (i, 0)),
        out_shape=jax.ShapeDtypeStruct((rows, d), F32),
        scratch_shapes=[pltpu.VMEM((tr, d), BF16), pltpu.VMEM((tr, d), F32)],
        compiler_params=_params(("arbitrary", "arbitrary")),
        name="ffn_final" if final_norm else "ffn",
    )(x, g, w_in, w_in, w_out, gf)


def _rope_cols(xs, cos, sin_signed, first_half):
    outs = []
    for c in range(xs.shape[1] // LANES):
        ch = xs[:, c * LANES:(c + 1) * LANES]
        rot = jnp.where(first_half, pltpu.roll(ch, LANES - HEAD_DIM // 2, axis=1),
                        pltpu.roll(ch, HEAD_DIM // 2, axis=1))
        outs.append(ch * cos + rot * sin_signed)
    return outs[0] if len(outs) == 1 else jnp.concatenate(outs, axis=1)


def _l1_proj_body(x_ref, g_ref, w_ref, wg_ref, cos_ref, sin_ref, *outs, prompt):
    x = x_ref[0] if prompt else x_ref[...]
    tr = x.shape[0]
    xn = _rms(x, g_ref[...]).astype(BF16)
    h = _dot(xn, w_ref[...])
    gates = jax.nn.sigmoid(_dot(xn, wg_ref[...]))
    cos = cos_ref[...]
    sin_signed = sin_ref[...]
    lane = lax.broadcasted_iota(jnp.int32, cos.shape, 1)
    first_half = (lane % HEAD_DIM) < (HEAD_DIM // 2)
    rope = functools.partial(_rope_cols, cos=cos, sin_signed=sin_signed, first_half=first_half)

    q = h[:, 0:D_MODEL] * Q_SCALE
    q_rot = rope(q)
    o = D_MODEL
    k_cmp_v_cmp = h[:, o:o + 2 * KV_W]
    k_slc = rope(h[:, o + 2 * KV_W:o + 3 * KV_W])
    v_slc = h[:, o + 3 * KV_W:o + 4 * KV_W]
    k_win = rope(h[:, o + 4 * KV_W:o + 5 * KV_W])
    v_win = h[:, o + 5 * KV_W:o + 6 * KV_W]

    if not prompt:
        q_ref, qr_ref, g_out, kv_ref, wn_ref = outs
        q_ref[...] = q
        qr_ref[...] = q_rot
        g_out[...] = gates
        kv_ref[:, 0:2 * KV_W] = k_cmp_v_cmp
        kv_ref[:, 2 * KV_W:3 * KV_W] = k_slc
        kv_ref[:, 3 * KV_W:4 * KV_W] = v_slc
        wn_ref[:, 0:KV_W] = k_win
        wn_ref[:, KV_W:2 * KV_W] = v_win
        return

    qt_ref, qrt_ref, gt_ref, kv_ref, wn_ref, kslc_ref, kwin_ref, vslct_ref, vwint_ref = outs
    kv_ref[0, :, 0:2 * KV_W] = k_cmp_v_cmp
    kv_ref[0, :, 2 * KV_W:3 * KV_W] = k_slc
    kv_ref[0, :, 3 * KV_W:4 * KV_W] = v_slc
    wn_ref[0, :, 0:KV_W] = k_win
    wn_ref[0, :, KV_W:2 * KV_W] = v_win
    qt = q.T.astype(BF16)
    qrt = q_rot.T.astype(BF16)
    gt = gates.T
    vst = v_slc.T.astype(BF16)
    vwt = v_win.T.astype(BF16)
    ones_rows = jnp.where(lax.broadcasted_iota(jnp.int32, (VX_ROWS - HEAD_DIM, QBLOCK), 0) == 0, 1.0, 0.0)
    ones_rows = ones_rows.astype(BF16)
    for r in range(tr // QBLOCK):
        cs = slice(r * QBLOCK, (r + 1) * QBLOCK)
        qt_ref[0, r] = qt[:, cs]
        qrt_ref[0, r] = qrt[:, cs]
        gt_ref[0, r] = gt[0:3 * N_HEADS, cs]
        for gi in range(N_KV):
            vslct_ref[0, gi, r] = jnp.concatenate([vst[gi * HEAD_DIM:(gi + 1) * HEAD_DIM, cs], ones_rows], axis=0)
            vwint_ref[0, gi, r] = vwt[gi * HEAD_DIM:(gi + 1) * HEAD_DIM, cs]
    xlane = lax.broadcasted_iota(jnp.int32, (tr, KX_W), 1)
    xblk = (lax.broadcasted_iota(jnp.int32, (tr, KX_W), 0) // SEL_BLOCK) % SLC_SBLK
    shift_col = HEAD_DIM + SLC_SBLK
    k_extra = jnp.where((xlane - HEAD_DIM == xblk) | (xlane == shift_col) | (xlane == shift_col + 1), 1.0, 0.0)
    k_pad = jnp.zeros((tr, KX_W - HEAD_DIM), F32)
    for gi in range(N_KV):
        k_g = jnp.concatenate([k_slc[:, gi * HEAD_DIM:(gi + 1) * HEAD_DIM], k_pad], axis=1)
        kslc_ref[0, gi] = (k_g + k_extra).astype(BF16)
        kwin_ref[0, gi] = k_win[:, gi * HEAD_DIM:(gi + 1) * HEAD_DIM].astype(BF16)


def _l1_proj_prompt(x, g, w_qkv, w_gate, cos, sin_signed, tr=512):
    b, t, d = x.shape
    nqb = t // QBLOCK
    rb = tr // QBLOCK
    const = lambda *shape: pl.BlockSpec(shape, lambda i, j: (0,) * len(shape))
    out_shape = [
        jax.ShapeDtypeStruct((b, nqb, D_MODEL, QBLOCK), BF16),
        jax.ShapeDtypeStruct((b, nqb, D_MODEL, QBLOCK), BF16),
        jax.ShapeDtypeStruct((b, nqb, 3 * N_HEADS, QBLOCK), F32),
        jax.ShapeDtypeStruct((b, t, 4 * KV_W), F32),
        jax.ShapeDtypeStruct((b, t, 2 * KV_W), F32),
        jax.ShapeDtypeStruct((b, N_KV, t, KX_W), BF16),
        jax.ShapeDtypeStruct((b, N_KV, t, HEAD_DIM), BF16),
        jax.ShapeDtypeStruct((b, N_KV, nqb, VX_ROWS, QBLOCK), BF16),
        jax.ShapeDtypeStruct((b, N_KV, nqb, HEAD_DIM, QBLOCK), BF16),
    ]
    out_specs = [
        pl.BlockSpec((1, rb, D_MODEL, QBLOCK), lambda i, j: (i, j, 0, 0)),
        pl.BlockSpec((1, rb, D_MODEL, QBLOCK), lambda i, j: (i, j, 0, 0)),
        pl.BlockSpec((1, rb, 3 * N_HEADS, QBLOCK), lambda i, j: (i, j, 0, 0)),
        pl.BlockSpec((1, tr, 4 * KV_W), lambda i, j: (i, j, 0)),
        pl.BlockSpec((1, tr, 2 * KV_W), lambda i, j: (i, j, 0)),
        pl.BlockSpec((1, N_KV, tr, KX_W), lambda i, j: (i, 0, j, 0)),
        pl.BlockSpec((1, N_KV, tr, HEAD_DIM), lambda i, j: (i, 0, j, 0)),
        pl.BlockSpec((1, N_KV, rb, VX_ROWS, QBLOCK), lambda i, j: (i, 0, j, 0, 0)),
        pl.BlockSpec((1, N_KV, rb, HEAD_DIM, QBLOCK), lambda i, j: (i, 0, j, 0, 0)),
    ]
    return pl.pallas_call(
        functools.partial(_l1_proj_body, prompt=True),
        grid=(b, t // tr),
        in_specs=[
            pl.BlockSpec((1, tr, d), lambda i, j: (i, j, 0)),
            const(1, d), const(*w_qkv.shape), const(*w_gate.shape),
            pl.BlockSpec((tr, LANES), lambda i, j: (j, 0)),
            pl.BlockSpec((tr, LANES), lambda i, j: (j, 0)),
        ],
        out_specs=out_specs,
        out_shape=out_shape,
        compiler_params=_params(("arbitrary", "arbitrary")),
        name="l1_proj_prompt",
    )(x, g, w_qkv, w_gate, cos, sin_signed)


def _l1_proj_sample(x, g, w_qkv, w_gate, cos, sin_signed):
    n, d = x.shape
    args = (x, g, w_qkv, w_gate, cos, sin_signed)
    widths = (D_MODEL, D_MODEL, LANES, 4 * KV_W, 2 * KV_W)
    return pl.pallas_call(
        functools.partial(_l1_proj_body, prompt=False),
        grid=(1,),
        in_specs=[pl.BlockSpec(a.shape, lambda i, nd=a.ndim: (0,) * nd) for a in args],
        out_specs=[pl.BlockSpec((n, w), lambda i: (0, 0)) for w in widths],
        out_shape=[jax.ShapeDtypeStruct((n, w), F32) for w in widths],
        compiler_params=_params(("arbitrary",)),
        name="l1_proj_sample",
    )(*args)


def _compress_prompt_body(x_ref, pe_ref, wk_ref, wv_ref, ck_ref, cvt_ref, acc_ref):
    l = pl.program_id(1)
    x = x_ref[0]
    parts = []
    for kind, w_ref in ((0, wk_ref), (1, wv_ref)):
        xs = x[:, kind * KV_W:(kind + 1) * KV_W]
        for half in range(2):
            pe = pe_ref[kind, pl.ds(half * CMP_STRIDE + l, 1), :]
            parts.append(_dot((xs + pe).astype(BF16), w_ref[half, 0]))

    @pl.when(l == 0)
    def _():
        for i, p in enumerate(parts):
            acc_ref[i] = p

    @pl.when(l > 0)
    def _():
        for i, p in enumerate(parts):
            acc_ref[i] += p

    @pl.when(l == pl.num_programs(1) - 1)
    def _():
        nseg = x.shape[0]
        ck = acc_ref[0] + pltpu.roll(acc_ref[1], nseg - 1, axis=0)
        cv = acc_ref[2] + pltpu.roll(acc_ref[3], nseg - 1, axis=0)
        cvt = cv.T
        for gi in range(N_KV):
            ck_ref[0, gi] = ck[:, gi * HEAD_DIM:(gi + 1) * HEAD_DIM].astype(BF16)
            cvt_ref[0, gi] = cvt[gi * HEAD_DIM:(gi + 1) * HEAD_DIM, :].astype(BF16)


def _compress_prompt(kv, pe_t, wk_bd, wv_bd):
    b, t, _ = kv.shape
    nseg = t // CMP_STRIDE
    kvs = kv.reshape(b, nseg, CMP_STRIDE * 4 * KV_W)
    return pl.pallas_call(
        _compress_prompt_body,
        grid=(b, CMP_STRIDE),
        in_specs=[
            pl.BlockSpec((1, nseg, 2 * KV_W), lambda i, l: (i, 0, 2 * l)),
            pl.BlockSpec(pe_t.shape, lambda i, l: (0, 0, 0)),
            pl.BlockSpec((2, 1, KV_W, KV_W), lambda i, l: (0, l, 0, 0)),
            pl.BlockSpec((2, 1, KV_W, KV_W), lambda i, l: (0, l, 0, 0)),
        ],
        out_specs=[
            pl.BlockSpec((1, N_KV, nseg, HEAD_DIM), lambda i, l: (i, 0, 0, 0)),
            pl.BlockSpec((1, N_KV, HEAD_DIM, nseg), lambda i, l: (i, 0, 0, 0)),
        ],
        out_shape=[jax.ShapeDtypeStruct((b, N_KV, nseg, HEAD_DIM), BF16),
                   jax.ShapeDtypeStruct((b, N_KV, HEAD_DIM, nseg), BF16)],
        scratch_shapes=[pltpu.VMEM((4, nseg, KV_W), F32)],
        compiler_params=_params(("arbitrary", "arbitrary")),
        name="compress_prompt",
    )(kvs, pe_t, wk_bd, wv_bd)


def _nsa_prompt_body(qt_ref, qrt_ref, gt_ref, ck_ref, cvt_ref, kslc_ref, vslct_ref, kwin_ref, vwint_ref,
                     ovt_ref, x_ref, wout_ref, y_ref, ot_ref, score_ref, s_ref, sel_ref, m_ref, shift_ref, acc_ref):
    i = pl.program_id(1)
    qs = i * QBLOCK
    nlane = HPG * QBLOCK
    nc = ck_ref.shape[2]
    ns = ovt_ref.shape[0]

    def head_cols(ref, g):
        return jnp.concatenate(
            [ref[0, 0, (g * HPG + h) * HEAD_DIM:(g * HPG + h + 1) * HEAD_DIM, :] for h in range(HPG)], axis=1)

    def gate_rows(g, branch):
        return jnp.concatenate(
            [gt_ref[0, 0, 3 * (g * HPG + h) + branch:3 * (g * HPG + h) + branch + 1, :] for h in range(HPG)], axis=1)

    def put_out(g, val, first):
        for h in range(HPG):
            rs = slice((g * HPG + h) * HEAD_DIM, (g * HPG + h + 1) * HEAD_DIM)
            piece = val[:, h * QBLOCK:(h + 1) * QBLOCK]
            ot_ref[rs, :] = piece if first else ot_ref[rs, :] + piece

    for g in range(N_KV):
        q_t = head_cols(qt_ref, g)

        sc = _dot(ck_ref[0, g], q_t)
        c_idx = lax.broadcasted_iota(jnp.int32, (nc, nlane), 0)
        qpos = qs + lax.broadcasted_iota(jnp.int32, (nc, nlane), 1) % QBLOCK
        cvalid = c_idx * CMP_STRIDE + (CMP_BLOCK - 1) <= qpos
        sm = jnp.where(cvalid, sc, NEG_INF)
        e = jnp.where(cvalid, jnp.exp2(sm - jnp.max(sm, axis=0, keepdims=True)), 0.0)
        den = jnp.sum(e, axis=0, keepdims=True)
        p_cmp = (e / jnp.where(den > 0.0, den, 1.0)).astype(BF16)
        o_cmp = _dot(cvt_ref[0, g], p_cmp)
        imp4 = _dot(ovt_ref[...], p_cmp)
        imp = (imp4[:, 0:QBLOCK] + imp4[:, QBLOCK:2 * QBLOCK]
               + imp4[:, 2 * QBLOCK:3 * QBLOCK] + imp4[:, 3 * QBLOCK:4 * QBLOCK])

        j_idx = lax.broadcasted_iota(jnp.int32, (ns, QBLOCK), 0)
        qp = qs + lax.broadcasted_iota(jnp.int32, (ns, QBLOCK), 1)
        cur = qp // SEL_BLOCK
        forced = (j_idx == 0) | (j_idx == cur) | (j_idx == cur - 1)
        svalid = j_idx * SEL_BLOCK <= qp
        score_ref[g] = jnp.where(svalid, imp + jnp.where(forced, FORCE_BONUS, 0.0), NEG_INF)
        put_out(g, gate_rows(g, 0) * o_cmp, True)

    j_idx = lax.broadcasted_iota(jnp.int32, (ns, QBLOCK), 0)

    def rank_step(jp, cnts):
        tie = jnp.where(jp < j_idx, 1, 0)
        out = []
        for g in range(N_KV):
            score = score_ref[g]
            rb = jnp.broadcast_to(score_ref[g, pl.ds(jp, 1), :], (ns, QBLOCK))
            out.append(cnts[g] + jnp.where(rb > score, 1, jnp.where(rb == score, tie, 0)))
        return tuple(out)

    n_cand = jnp.minimum(2 * i + 2, ns)
    cnts = lax.fori_loop(0, n_cand, rank_step, (jnp.zeros((ns, QBLOCK), jnp.int32),) * N_KV)
    for g in range(N_KV):
        sel_ref[g] = jnp.concatenate([jnp.where(cnts[g] < TOP_N, 0.0, NEG_INF)] * HPG, axis=1)

    kbs = SLC_KEYS // QBLOCK
    last = qs // SLC_KEYS
    key_rel = (lax.broadcasted_iota(jnp.int32, (SLC_KEYS, nlane), 0)
               - lax.broadcasted_iota(jnp.int32, (SLC_KEYS, nlane), 1) % QBLOCK)
    causal_last = key_rel <= qs - last * SLC_KEYS
    zero_rows = jnp.zeros((KX_W - HEAD_DIM - SLC_SBLK, nlane), F32)
    n_slot = s_ref.shape[0]

    def max_part(g, kb, causal):
        k0 = pl.multiple_of(kb * SLC_KEYS, SLC_KEYS)
        bias = sel_ref[g, pl.ds(pl.multiple_of(SLC_SBLK * kb, SLC_SBLK), SLC_SBLK), :]
        extra = jnp.concatenate([bias, zero_rows], axis=0).astype(BF16)
        qx = jnp.concatenate([head_cols(qrt_ref, g), extra], axis=0)
        s = _dot(kslc_ref[0, g, pl.ds(k0, SLC_KEYS), :], qx)
        if causal:
            s = jnp.where(causal_last, s, NEG_INF)
        s_ref[g % n_slot, pl.ds(k0, SLC_KEYS), :] = s
        m_ref[g] = jnp.maximum(m_ref[g], jnp.max(s.reshape(-1, SUBLANES, nlane), axis=0))

    def sum_part(g, kb):
        k0 = pl.multiple_of(kb * SLC_KEYS, SLC_KEYS)
        p = jnp.exp2(s_ref[g % n_slot, pl.ds(k0, SLC_KEYS), :] - shift_ref[g, 0:1, :]).astype(BF16)
        vt = jnp.concatenate([vslct_ref[0, g, kbs * kb + j] for j in range(kbs)], axis=1)
        acc_ref[g] += _dot(vt, p)

    for ga, gb in [(g if g < N_KV else None, g - 1 if g > 0 else None) for g in range(N_KV + 1)]:
        if ga is not None:
            m_ref[ga] = jnp.full(m_ref.shape[1:], NEG_INF, F32)
        if gb is not None:
            acc_ref[gb] = jnp.zeros(acc_ref.shape[1:], F32)

        def step(kb, causal, ga=ga, gb=gb):
            if ga is not None:
                max_part(ga, kb, causal)
            if gb is not None:
                sum_part(gb, kb)

        step(last, True)
        lax.fori_loop(0, last, lambda kb, c, step=step: (step(kb, False), c)[1], 0)
        if ga is not None:
            m = jnp.max(m_ref[ga], axis=0, keepdims=True)
            shift_ref[ga] = jnp.broadcast_to(m, shift_ref.shape[1:])
        if gb is not None:
            acc = acc_ref[gb]
            put_out(gb, gate_rows(gb, 1) * (acc[0:HEAD_DIM] / acc[HEAD_DIM:HEAD_DIM + 1]), False)

    nwb = WINDOW // QBLOCK + 1
    kr = lax.broadcasted_iota(jnp.int32, (QBLOCK, nlane), 0)
    qc = lax.broadcasted_iota(jnp.int32, (QBLOCK, nlane), 1) % QBLOCK
    for g in range(N_KV):
        qr_t = head_cols(qrt_ref, g)
        s_parts, v_parts = [], []
        for j in range(nwb):
            kb = i - (nwb - 1) + j
            kbc = jnp.maximum(kb, 0)
            s = _dot(kwin_ref[0, g, pl.ds(pl.multiple_of(kbc * QBLOCK, QBLOCK), QBLOCK), :], qr_t)
            if j == 0:
                s = jnp.where(kr > qc, s, NEG_INF)
            if j == nwb - 1:
                s = jnp.where(kr <= qc, s, NEG_INF)
            else:
                s = jnp.where(kb >= 0, s, NEG_INF)
            s_parts.append(s)
            v_parts.append(vwint_ref[0, g, kbc])
        s = jnp.concatenate(s_parts, axis=0)
        p = jnp.exp2(s - jnp.max(s, axis=0, keepdims=True))
        o_win = _dot(jnp.concatenate(v_parts, axis=1), p.astype(BF16)) / jnp.sum(p, axis=0, keepdims=True)
        put_out(g, gate_rows(g, 2) * o_win, False)

    o = ot_ref[...].T.astype(BF16)
    y_ref[0] = x_ref[0] + _dot(o, wout_ref[...])


def _nsa_prompt(qt, qrt, gt, ck, cvt, kslc, vslct, kwin, vwint, ovt, x, w_out):
    b, t, d = x.shape
    nqb = t // QBLOCK
    ns = ovt.shape[0]
    whole = lambda a: pl.BlockSpec((1,) + a.shape[1:], lambda i, j, nd=a.ndim: (i,) + (0,) * (nd - 1),
                                   pipeline_mode=pl.Buffered(1))
    return pl.pallas_call(
        _nsa_prompt_body,
        grid=(b, nqb),
        in_specs=[
            pl.BlockSpec((1, 1, D_MODEL, QBLOCK), lambda i, j: (i, j, 0, 0)),
            pl.BlockSpec((1, 1, D_MODEL, QBLOCK), lambda i, j: (i, j, 0, 0)),
            pl.BlockSpec((1, 1, 3 * N_HEADS, QBLOCK), lambda i, j: (i, j, 0, 0)),
            whole(ck), whole(cvt), whole(kslc), whole(vslct), whole(kwin), whole(vwint),
            pl.BlockSpec(ovt.shape, lambda i, j: (0, 0), pipeline_mode=pl.Buffered(1)),
            pl.BlockSpec((1, QBLOCK, d), lambda i, j: (i, j, 0)),
            pl.BlockSpec(w_out.shape, lambda i, j: (0, 0), pipeline_mode=pl.Buffered(1)),
        ],
        out_specs=pl.BlockSpec((1, QBLOCK, d), lambda i, j: (i, j, 0)),
        out_shape=jax.ShapeDtypeStruct((b, t, d), F32),
        scratch_shapes=[pltpu.VMEM((D_MODEL, QBLOCK), F32),
                        pltpu.VMEM((N_KV, ns, QBLOCK), F32),
                        pltpu.VMEM((SLC_PAR, t, HPG * QBLOCK), F32),
                        pltpu.VMEM((N_KV, ns, HPG * QBLOCK), F32),
                        pltpu.VMEM((N_KV, SUBLANES, HPG * QBLOCK), F32),
                        pltpu.VMEM((N_KV, SUBLANES, HPG * QBLOCK), F32),
                        pltpu.VMEM((N_KV, VX_ROWS, HPG * QBLOCK), F32)],
        compiler_params=_params(("arbitrary", "arbitrary")),
        name="nsa_prompt_attn",
    )(qt, qrt, gt, ck, cvt, kslc, vslct, kwin, vwint, ovt, x, w_out)


def _softmax_rows(s_past, valid_past, s_new, valid_new):
    s_past = jnp.where(valid_past, s_past, NEG_INF)
    s_new = jnp.where(valid_new, s_new, NEG_INF)
    m = jnp.maximum(jnp.max(s_past, axis=1, keepdims=True), s_new)
    p = jnp.exp2(s_past - m)
    p_new = jnp.exp2(s_new - m)
    l = jnp.sum(p, axis=1, keepdims=True) + p_new
    return p, p_new, l


def _nsa_sample_body(pt_ref, cache_ref, cwin_ref, qbd_ref, qrbd_ref, kvn_ref, wnn_ref, gates_ref,
                     pe_ref, wk_ref, wv_ref, ov_ref, exp_ref, o_ref, buf_ref, xs_ref, sem_ref,
                     *, past_len, n_pages):
    n = pl.program_id(0)
    nseq = pl.num_programs(0)
    slot = n % 2

    def page_copy(seq, p, sl):
        return pltpu.make_async_copy(cache_ref.at[pt_ref[seq, p]], buf_ref.at[sl, p], sem_ref.at[sl])

    @pl.when(n == 0)
    def _():
        for p in range(n_pages):
            page_copy(0, p, 0).start()

    @pl.when(n + 1 < nseq)
    def _():
        for p in range(n_pages):
            page_copy(n + 1, p, 1 - slot).start()

    for p in range(n_pages):
        page_copy(n, p, slot).wait()

    def cached_t(kind):
        return jnp.concatenate(
            [buf_ref[slot, p, pl.ds(kind * KV_W, KV_W), :] for p in range(n_pages)], axis=1).astype(BF16)

    nseg = past_len // CMP_STRIDE
    halves = KV_W // LANES

    for kind in range(2):
        for p in range(n_pages):
            for c in range(halves):
                xt = buf_ref[slot, p, pl.ds(kind * KV_W + c * LANES, LANES), :]
                xs_ref[kind, c, p * PAGE_SIZE:(p + 1) * PAGE_SIZE, :] = xt.T
    kvn = kvn_ref[0]
    wnn = wnn_ref[0]
    row8 = lax.broadcasted_iota(jnp.int32, (SUBLANES, KV_W), 0)

    cmp = []
    for kind, w_ref in ((0, wk_ref), (1, wv_ref)):
        lo = jnp.zeros((nseg + SUBLANES, KV_W), F32)
        hi = jnp.zeros((nseg + SUBLANES, KV_W), F32)
        new_row = kvn[:, kind * KV_W:(kind + 1) * KV_W]
        for l in range(CMP_STRIDE):
            xs = jnp.concatenate(
                [xs_ref[kind, c, pl.ds(l, nseg, stride=CMP_STRIDE), :] for c in range(halves)], axis=1)
            tail = jnp.where(row8 == 0, new_row, 0.0) if l == 0 else jnp.zeros((SUBLANES, KV_W), F32)
            xs = jnp.concatenate([xs, tail], axis=0)
            lo = lo + _dot((xs + pe_ref[kind, l:l + 1, :]).astype(BF16), w_ref[0, l])
            hi = hi + _dot((xs + pe_ref[kind, CMP_STRIDE + l:CMP_STRIDE + l + 1, :]).astype(BF16), w_ref[1, l])
        blocks = lo + pltpu.roll(hi, nseg + SUBLANES - 1, axis=0)
        cmp.append(blocks[0:nseg].astype(BF16))
    ck, cv = cmp
    nc = nseg
    qpos = past_len

    qbd = qbd_ref[0]
    qrbd = qrbd_ref[0]
    c_idx = lax.broadcasted_iota(jnp.int32, (N_HEADS, nc), 1)
    cvalid = c_idx * CMP_STRIDE + (CMP_BLOCK - 1) <= qpos
    sm = jnp.where(cvalid, _dot_nt(qbd, ck), NEG_INF)
    e = jnp.where(cvalid, jnp.exp2(sm - jnp.max(sm, axis=1, keepdims=True)), 0.0)
    den = jnp.sum(e, axis=1, keepdims=True)
    p_cmp = (e / jnp.where(den > 0.0, den, 1.0)).astype(BF16)
    o_cmp = _dot(p_cmp, cv)
    imp16 = _dot(p_cmp, ov_ref[...])

    nsl = ov_ref.shape[1]
    j_row = lax.broadcasted_iota(jnp.int32, (nsl, nsl), 1)
    j_col = lax.broadcasted_iota(jnp.int32, (nsl, nsl), 0)
    j1 = lax.broadcasted_iota(jnp.int32, (1, nsl), 1)
    cur = qpos // SEL_BLOCK
    forced = (j1 == 0) | (j1 == cur) | (j1 == cur - 1)
    svalid = j1 * SEL_BLOCK <= qpos
    sel_rows = []
    for g in range(N_KV):
        imp = (imp16[g * HPG:g * HPG + 1] + imp16[g * HPG + 1:g * HPG + 2]
               + imp16[g * HPG + 2:g * HPG + 3] + imp16[g * HPG + 3:g * HPG + 4])
        score = jnp.where(svalid, imp + jnp.where(forced, FORCE_BONUS, 0.0), NEG_INF)
        s_b = jnp.broadcast_to(score, (nsl, nsl))
        s_t = jnp.sum(jnp.where(j_row == j_col, s_b, 0.0), axis=1, keepdims=True)
        beats = (s_t > s_b) | ((s_t == s_b) & (j_col < j_row))
        rank = jnp.sum(jnp.where(beats, 1.0, 0.0), axis=0, keepdims=True)
        sel = jnp.where(rank < TOP_N, 1.0, 0.0)
        sel_rows.append(jnp.broadcast_to(sel, (HPG, nsl)))
    sel16 = jnp.concatenate(sel_rows, axis=0)
    key_sel = _dot(sel16.astype(BF16), exp_ref[...]) > 0.5
    new_sel = sel16[:, cur:cur + 1] > 0.5

    def new_key_score(k_new):
        return jnp.sum(qrbd.astype(F32) * k_new.astype(BF16).astype(F32), axis=1, keepdims=True)

    def new_val(p_new, v_new):
        return p_new.astype(BF16).astype(F32) * v_new.astype(BF16).astype(F32)

    p, p_new, l = _softmax_rows(_dot(qrbd, cached_t(2)), key_sel,
                                new_key_score(kvn[:, 2 * KV_W:3 * KV_W]), new_sel)
    o_slc = (_dot_nt(p.astype(BF16), cached_t(3)) + new_val(p_new, kvn[:, 3 * KV_W:4 * KV_W])) / l

    wb = cwin_ref.shape[2]
    k_win_t = cwin_ref[0, 0:KV_W, :].astype(BF16)
    v_win_t = cwin_ref[0, KV_W:2 * KV_W, :].astype(BF16)
    wpos = past_len - wb + lax.broadcasted_iota(jnp.int32, (N_HEADS, wb), 1)
    dist = qpos - wpos
    wvalid = (wpos >= 0) & (dist >= 0) & (dist < WINDOW)
    p, p_new, l = _softmax_rows(_dot(qrbd, k_win_t), wvalid,
                                new_key_score(wnn[:, 0:KV_W]), jnp.full((N_HEADS, 1), True))
    o_win = (_dot_nt(p.astype(BF16), v_win_t) + new_val(p_new, wnn[:, KV_W:2 * KV_W])) / l

    o = gates_ref[0, 0] * o_cmp + gates_ref[0, 1] * o_slc + gates_ref[0, 2] * o_win
    hrow = lax.broadcasted_iota(jnp.int32, (N_HEADS, KV_W), 0)
    gcol = lax.broadcasted_iota(jnp.int32, (N_HEADS, KV_W), 1)
    o_ref[0] = jnp.where(hrow // HPG == gcol // HEAD_DIM, o, 0.0)


def _nsa_sample(page_table, cache, cwin, qbd, qrbd, kvn, wnn, gates, pe_t, wk_bd, wv_bd, ov, expand):
    nseq, n_pages = page_table.shape
    past_len = n_pages * PAGE_SIZE
    blk = lambda a: pl.BlockSpec((1,) + a.shape[1:], lambda i, pt, nd=a.ndim: (i,) + (0,) * (nd - 1))
    full = lambda a: pl.BlockSpec(a.shape, lambda i, pt, nd=a.ndim: (0,) * nd)
    grid_spec = pltpu.PrefetchScalarGridSpec(
        num_scalar_prefetch=1,
        grid=(nseq,),
        in_specs=[
            pl.BlockSpec(memory_space=pl.ANY),
            blk(cwin), blk(qbd), blk(qrbd), blk(kvn), blk(wnn), blk(gates),
            full(pe_t), full(wk_bd), full(wv_bd), full(ov), full(expand),
        ],
        out_specs=pl.BlockSpec((1, N_HEADS, KV_W), lambda i, pt: (i, 0, 0)),
        scratch_shapes=[pltpu.VMEM((2, n_pages) + cache.shape[1:], F32),
                        pltpu.VMEM((2, KV_W // LANES, past_len, LANES), F32),
                        pltpu.SemaphoreType.DMA((2,))],
    )
    return pl.pallas_call(
        functools.partial(_nsa_sample_body, past_len=past_len, n_pages=n_pages),
        grid_spec=grid_spec,
        out_shape=jax.ShapeDtypeStruct((nseq, N_HEADS, KV_W), F32),
        compiler_params=_params(("arbitrary",)),
        name="nsa_sample_attn",
    )(page_table, cache, cwin, qbd, qrbd, kvn, wnn, gates, pe_t, wk_bd, wv_bd, ov, expand)


def _out_proj_body(o_ref, w_ref, x_ref, y_ref):
    y_ref[...] = x_ref[...] + _dot(o_ref[...].astype(BF16), w_ref[...])


def _out_proj(o, w, x):
    args = (o, w, x)
    return pl.pallas_call(
        _out_proj_body,
        grid=(1,),
        in_specs=[pl.BlockSpec(a.shape, lambda i: (0, 0)) for a in args],
        out_specs=pl.BlockSpec(x.shape, lambda i: (0, 0)),
        out_shape=jax.ShapeDtypeStruct(x.shape, F32),
        compiler_params=_params(("arbitrary",)),
        name="out_proj_sample",
    )(*args)


def _rope_tables(pos):
    half = HEAD_DIM // 2
    inv = ROPE_THETA ** (-jnp.arange(half, dtype=F32) * 2.0 / HEAD_DIM)
    ang = pos.astype(F32)[:, None] * inv[None, :]
    cos = jnp.concatenate([jnp.cos(ang), jnp.cos(ang)], -1)
    sin_signed = jnp.concatenate([-jnp.sin(ang), jnp.sin(ang)], -1)
    reps = LANES // HEAD_DIM
    return jnp.tile(cos, (1, reps)), jnp.tile(sin_signed, (1, reps))


def _block_diag_cmp(w):
    eye = jnp.eye(N_KV, dtype=w.dtype)
    bd = jnp.einsum('gh,lde->lgdhe', eye, w).reshape(CMP_BLOCK, KV_W, KV_W)
    return bd.reshape(2, CMP_STRIDE, KV_W, KV_W).astype(BF16)


def _overlap(nc, ns):
    cstart = np.arange(nc) * CMP_STRIDE
    sstart = np.arange(ns) * SEL_BLOCK
    ov = ((cstart[:, None] < sstart[None, :] + SEL_BLOCK)
          & (cstart[:, None] + CMP_BLOCK > sstart[None, :]))
    return ov.astype(np.float32)


def _row(v):
    return v.reshape(1, -1)


def _ffn_weights(w_ffn_in, w_ffn_out):
    return w_ffn_in.astype(BF16), w_ffn_out.astype(BF16)


def _ffn_rows(x, g, ffn_w, layer, g_final, final_norm):
    shape = x.shape
    x2 = x.reshape(-1, shape[-1])
    tr = min(512, x2.shape[0])
    y = _ffn(x2, _row(g), ffn_w[0][layer], ffn_w[1][layer], _row(g_final), final_norm, tr=tr)
    return y.reshape(shape)


def _attn_weights(w_in1, pe_cmp, w_cmp, w_out1):
    d = w_in1.shape[0]
    q_dim = N_HEADS * HEAD_DIM
    kv_dim = 6 * KV_W
    own = (jnp.arange(N_HEADS)[:, None] // HPG == jnp.arange(N_KV)[None, :]).astype(F32)
    w_out1_exp = (w_out1.reshape(N_HEADS, 1, HEAD_DIM, d) * own[:, :, None, None]).reshape(N_HEADS * KV_W, d)
    return dict(
        w_qkv=w_in1[:, :q_dim + kv_dim].astype(BF16),
        w_gate=jnp.zeros((d, LANES), F32).at[:, :3 * N_HEADS].set(w_in1[:, q_dim + kv_dim:]).astype(BF16),
        w_out1=w_out1.astype(BF16),
        w_out1_exp=w_out1_exp.astype(BF16),
        wk_bd=_block_diag_cmp(w_cmp[0]),
        wv_bd=_block_diag_cmp(w_cmp[1]),
        pe_t=jnp.tile(pe_cmp, (1, 1, N_KV)),
        own=own,
    )


def _layer0_mixer(x_prompt, xs, state_conv, g, w_in0, conv_w, norm_v, w_spatial, b_spatial, w_out0):
    w_in0_b = w_in0.astype(BF16)
    w_out0_b = w_out0.astype(BF16)
    cw8 = jnp.zeros((SUBLANES, D_CONV), F32).at[0:conv_w.shape[0]].set(conv_w)
    causal = jnp.tril(jnp.ones((CHUNK, CHUNK), dtype=bool))
    wsp = jnp.where(causal[None], w_spatial, 0).astype(BF16)
    per_head = D_CHUNK // N_CHUNK_HEADS
    bsp = jnp.repeat(b_spatial.T, per_head, axis=1)
    w00 = _row(jnp.repeat(w_spatial[:, 0, 0], per_head))
    b0 = _row(jnp.repeat(b_spatial[:, 0], per_head))
    tr = min(512, x_prompt.shape[1])
    xp, ztail = _l0_prompt(x_prompt, _row(g), w_in0_b, cw8, _row(norm_v), wsp, bsp, w_out0_b, tr=tr)
    conv_state_prompt = ztail[:, SUBLANES - (conv_w.shape[0] - 1):]
    xs, z_s, vn_s = _l0_sample(xs, state_conv[:, 0], state_conv[:, 1], _row(g), w_in0_b, cw8,
                               _row(norm_v), w00, b0, w_out0_b)
    conv_state_sample = jnp.concatenate([state_conv[:, 1:], z_s[:, None]], axis=1)
    return xp, xs, conv_state_prompt, conv_state_sample, vn_s[:, None]


def _layer1_prompt(xp, g, aw):
    b, t, d = xp.shape
    cos_p, sin_p = _rope_tables(jnp.arange(t))
    tr = min(512, t)
    qt, qrt, gt, kv_p, win_p, kslc, kwin, vslct, vwint = _l1_proj_prompt(
        xp, _row(g), aw["w_qkv"], aw["w_gate"], cos_p, sin_p, tr=tr)
    ck, cvt = _compress_prompt(kv_p, aw["pe_t"], aw["wk_bd"], aw["wv_bd"])
    nseg = t // CMP_STRIDE
    ns = -(-t // SEL_BLOCK)
    ovt = jnp.asarray(_overlap(nseg, ns).T, dtype=BF16)
    xp = _nsa_prompt(qt, qrt, gt, ck, cvt, kslc, vslct, kwin, vwint, ovt, xp, aw["w_out1"])
    kv_prompt = kv_p.reshape(b, t, 4, N_KV, HEAD_DIM)
    wlen = min(WINDOW, t)
    win_prompt = win_p[:, t - wlen:].reshape(b, wlen, 2, N_KV, HEAD_DIM)
    return xp, kv_prompt, win_prompt


def _layer1_sample(xs, cache_kv, cache_win, page_table, g, aw):
    n, d = xs.shape
    past_len = page_table.shape[1] * PAGE_SIZE
    cos_s, sin_s = _rope_tables(jnp.full((n,), past_len))
    q_s, qr_s, gates_s, kv_s, win_s = _l1_proj_sample(xs, _row(g), aw["w_qkv"], aw["w_gate"], cos_s, sin_s)
    own = aw["own"]

    def group_expand(q):
        qh = q.reshape(n, N_HEADS, 1, HEAD_DIM) * own[None, :, :, None]
        return qh.reshape(n, N_HEADS, KV_W).astype(BF16)

    gates_l = gates_s[:, :3 * N_HEADS].reshape(n, N_HEADS, 3).transpose(0, 2, 1)[..., None]
    nc_s = past_len // CMP_STRIDE
    ns_s = -(-(past_len + 1) // SEL_BLOCK)
    ov_s = np.zeros((nc_s, LANES), np.float32)
    ov_s[:, :ns_s] = _overlap(nc_s, ns_s)
    expand = (np.arange(LANES)[:, None] == (np.arange(past_len) // SEL_BLOCK)[None, :]).astype(np.float32)
    o_bd = _nsa_sample(
        page_table, cache_kv.transpose(0, 2, 3, 4, 1).reshape(cache_kv.shape[0], 4 * KV_W, PAGE_SIZE),
        cache_win.transpose(0, 2, 3, 4, 1).reshape(n, 2 * KV_W, cache_win.shape[1]),
        group_expand(q_s), group_expand(qr_s), kv_s[:, None], win_s[:, None], gates_l,
        aw["pe_t"], aw["wk_bd"], aw["wv_bd"], jnp.asarray(ov_s, dtype=BF16), jnp.asarray(expand, dtype=BF16))
    xs = _out_proj(o_bd.reshape(n, N_HEADS * KV_W), aw["w_out1_exp"], xs)
    return xs, kv_s.reshape(n, 1, 4, N_KV, HEAD_DIM), win_s.reshape(n, 1, 2, N_KV, HEAD_DIM)


def kernel(x_prompt, x_sample, state_conv, cache_kv, cache_win, page_table, norm_mix, norm_ffn, norm_final,
           w_in0, conv_w, norm_v, w_spatial, b_spatial, w_out0, w_in1, pe_cmp, w_cmp, w_out1,
           w_ffn_in, w_ffn_out):
    b, t, d = x_prompt.shape
    n = x_sample.shape[0]
    xs = x_sample.reshape(n, d)
    ffn_w = _ffn_weights(w_ffn_in, w_ffn_out)
    attn_w = _attn_weights(w_in1, pe_cmp, w_cmp, w_out1)

    xp, xs, conv_state_prompt, conv_state_sample, chunk_v_sample = _layer0_mixer(
        x_prompt, xs, state_conv, norm_mix[0], w_in0, conv_w, norm_v, w_spatial, b_spatial, w_out0)
    xp = _ffn_rows(xp, norm_ffn[0], ffn_w, 0, norm_final, False)
    xs = _ffn_rows(xs, norm_ffn[0], ffn_w, 0, norm_final, False)

    xp, kv_prompt, win_prompt = _layer1_prompt(xp, norm_mix[1], attn_w)
    xs, kv_sample, win_sample = _layer1_sample(xs, cache_kv, cache_win, page_table, norm_mix[1], attn_w)
    y_prompt = _ffn_rows(xp, norm_ffn[1], ffn_w, 1, norm_final, True)
    y_sample = _ffn_rows(xs, norm_ffn[1], ffn_w, 1, norm_final, True)
    return (y_prompt, y_sample.reshape(n, 1, d), conv_state_prompt, conv_state_sample, chunk_v_sample,
            kv_prompt, kv_sample, win_prompt, win_sample)
```
